```python
import jax, jax.numpy as jnp
from jax import lax
import numpy as np

D_MODEL = 1024
BATCH = 8
SEQ = 2048
DEPTH = 1
DEC_BATCH = 128
DEC_SEQ = 4
PAST_LEN = 16384
PAGE_SIZE = 128

RET_HEADS = 4
RET_DK = 128
RET_DV = 128
GLA_HEADS = 4
GLA_DK = 64
GLA_DV = 128
GLA_GATE_RANK = 16
GLA_GATE_TAU = 16.0
D_FF = 2816
CONV_WIDTH = 3
CHUNK = 64
ROPE_BASE = 10000.0
EPS = 1e-6

SPLIT_SIZES = (RET_HEADS * RET_DK, RET_HEADS * RET_DK, RET_HEADS * RET_DV, RET_HEADS * RET_DV,
               GLA_HEADS * GLA_DK, GLA_HEADS * GLA_DK, GLA_HEADS * GLA_DV, GLA_HEADS * GLA_DV,
               GLA_GATE_RANK, D_MODEL, D_MODEL)
D_IN = (2 * RET_HEADS * RET_DK + 2 * RET_HEADS * RET_DV + 2 * GLA_HEADS * GLA_DK
        + 2 * GLA_HEADS * GLA_DV + GLA_GATE_RANK + 2 * D_MODEL)

kernel_name = 'hybrid_retention_gla_convffn_step'


def rmsnorm(x, g):
    xf = x.astype(jnp.float32)
    y = xf * lax.rsqrt(jnp.mean(xf * xf, axis=-1, keepdims=True) + EPS)
    return (y * g.astype(jnp.float32)).astype(x.dtype)


def group_rmsnorm(o, g):
    B, L, H, dv = o.shape
    of = o.astype(jnp.float32)
    y = of * lax.rsqrt(jnp.mean(of * of, axis=-1, keepdims=True) + EPS)
    y = y.reshape(B, L, H * dv) * g.astype(jnp.float32)
    return y.astype(o.dtype)


def rotary(t, pos):
    half = t.shape[-1] // 2
    inv = ROPE_BASE ** (-jnp.arange(half, dtype=jnp.float32) / half)
    ang = pos[:, None] * inv[None, :]
    cos = jnp.cos(ang)[None, :, None, :]
    sin = jnp.sin(ang)[None, :, None, :]
    tf = t.astype(jnp.float32)
    t1, t2 = tf[..., :half], tf[..., half:]
    out = jnp.concatenate([t1 * cos - t2 * sin, t1 * sin + t2 * cos], axis=-1)
    return out.astype(t.dtype)


def chunked_gated_linear(q, k, v, log_a, s0):
    B, L, H, dk = q.shape
    dv = v.shape[-1]
    C = CHUNK if L % CHUNK == 0 else L
    n = L // C

    def to_chunks(t):
        return t.astype(jnp.float32).reshape(B, n, C, H, t.shape[-1]).transpose(1, 0, 3, 2, 4)

    qc, kc, vc, ac = to_chunks(q), to_chunks(k), to_chunks(v), to_chunks(log_a)
    mask = jnp.tril(jnp.ones((C, C), dtype=bool))

    def step(S, inp):
        qi, ki, vi, ai = inp
        b = jnp.cumsum(ai, axis=2)
        b_last = b[:, :, -1:, :]
        q_dec = qi * jnp.exp(b)
        k_dec = ki * jnp.exp(-b)
        scores = jnp.einsum('bhid,bhjd->bhij', q_dec, k_dec)
        scores = jnp.where(mask, scores, 0.0)
        o = (jnp.einsum('bhij,bhjv->bhiv', scores, vi)
             + jnp.einsum('bhid,bhdv->bhiv', q_dec, S))
        k_tail = ki * jnp.exp(b_last - b)
        S_new = (jnp.exp(b_last[:, :, 0, :])[..., None] * S
                 + jnp.einsum('bhjd,bhjv->bhdv', k_tail, vi))
        return S_new, o

    S, o = lax.scan(step, s0.astype(jnp.float32), (qc, kc, vc, ac))
    o = o.transpose(1, 0, 3, 2, 4).reshape(B, L, H, dv)
    return o.astype(v.dtype), S.astype(s0.dtype)


def causal_dwconv(u, prev, w, b):
    L = u.shape[1]
    ext = jnp.concatenate([prev.astype(u.dtype), u], axis=1)
    out = b + ext[:, 0:L] * w[0]
    for i in range(1, CONV_WIDTH):
        out = out + ext[:, i:i + L] * w[i]
    return out, ext[:, -(CONV_WIDTH - 1):]


def split_cols(p):
    outs, start = [], 0
    for s in SPLIT_SIZES:
        outs.append(p[..., start:start + s])
        start += s
    return outs


def layer(x, pos, s_ret, s_gla, conv_prev, ln1, w_in, w_gate_up, b_gate_up, g_ret, g_gla,
          w_ret_out, w_gla_out, w_o, ln2, w_up, conv_w, conv_b, w_down):
    B, L, _ = x.shape
    n = rmsnorm(x, ln1)
    proj = n @ w_in
    rq, rk, rv, rg, gq, gk, gv, gg, ga, m_r, m_g = split_cols(proj)

    rq = rotary(rq.reshape(B, L, RET_HEADS, RET_DK), pos)
    rk = rotary(rk.reshape(B, L, RET_HEADS, RET_DK), pos) * (RET_DK ** -0.5)
    rv = rv.reshape(B, L, RET_HEADS, RET_DV)
    log_gamma = jnp.log(1.0 - 2.0 ** (-5.0 - jnp.arange(RET_HEADS, dtype=jnp.float32)))
    log_g = jnp.broadcast_to(log_gamma[None, None, :, None], (B, L, RET_HEADS, RET_DK))
    o_r, s_ret_new = chunked_gated_linear(rq, rk, rv, log_g, s_ret)
    y_r = (group_rmsnorm(o_r, g_ret) * jax.nn.silu(rg)) @ w_ret_out

    z = (ga @ w_gate_up + b_gate_up).astype(jnp.float32)
    log_a = (jax.nn.log_sigmoid(z) / GLA_GATE_TAU).reshape(B, L, GLA_HEADS, GLA_DK)
    gq = gq.reshape(B, L, GLA_HEADS, GLA_DK) * (GLA_DK ** -0.5)
    gk = gk.reshape(B, L, GLA_HEADS, GLA_DK)
    gv = gv.reshape(B, L, GLA_HEADS, GLA_DV)
    o_g, s_gla_new = chunked_gated_linear(gq, gk, gv, log_a, s_gla)
    y_g = (group_rmsnorm(o_g, g_gla) * jax.nn.silu(gg)) @ w_gla_out

    mix = jax.nn.sigmoid(m_r) * y_r + jax.nn.sigmoid(m_g) * y_g
    h = x + mix @ w_o

    n2 = rmsnorm(h, ln2)
    up = n2 @ w_up
    u, vv = up[..., :D_FF], up[..., D_FF:]
    uc, conv_tail = causal_dwconv(u, conv_prev, conv_w, conv_b)
    h = h + (jax.nn.gelu(uc, approximate=False) * vv) @ w_down
    return h, s_ret_new, s_gla_new, conv_tail


def setup_inputs(seed: int = 0) -> dict:
    key = jax.random.key(seed)
    ks = jax.random.split(key, 24)
    f32 = jnp.float32

    def nrm(k, shape, scale):
        return jax.random.normal(k, shape, f32) * scale

    def gain(k, shape):
        return 1.0 + 0.02 * jax.random.normal(k, shape, f32)

    return {
        'x_prompt': nrm(ks[0], (BATCH, SEQ, D_MODEL), 1.0),
        'x_sample': nrm(ks[1], (DEC_BATCH, DEC_SEQ, D_MODEL), 1.0),
        'state_ret': nrm(ks[2], (DEPTH, DEC_BATCH, RET_HEADS, RET_DK, RET_DV), 0.3),
        'state_gla': nrm(ks[3], (DEPTH, DEC_BATCH, GLA_HEADS, GLA_DK, GLA_DV), 0.3),
        'cache_conv': nrm(ks[4], (DEPTH, DEC_BATCH, CONV_WIDTH - 1, D_FF), 1.0),
        'ln1': gain(ks[5], (DEPTH, D_MODEL)),
        'w_in': nrm(ks[6], (DEPTH, D_MODEL, D_IN), D_MODEL ** -0.5),
        'w_gate_up': nrm(ks[7], (DEPTH, GLA_GATE_RANK, GLA_HEADS * GLA_DK), GLA_GATE_RANK ** -0.5),
        'b_gate_up': nrm(ks[8], (DEPTH, GLA_HEADS * GLA_DK), 0.1),
        'g_ret': gain(ks[9], (DEPTH, RET_HEADS * RET_DV)),
        'g_gla': gain(ks[10], (DEPTH, GLA_HEADS * GLA_DV)),
        'w_ret_out': nrm(ks[11], (DEPTH, RET_HEADS * RET_DV, D_MODEL), (RET_HEADS * RET_DV) ** -0.5),
        'w_gla_out': nrm(ks[12], (DEPTH, GLA_HEADS * GLA_DV, D_MODEL), (GLA_HEADS * GLA_DV) ** -0.5),
        'w_o': nrm(ks[13], (DEPTH, D_MODEL, D_MODEL), D_MODEL ** -0.5),
        'ln2': gain(ks[14], (DEPTH, D_MODEL)),
        'w_up': nrm(ks[15], (DEPTH, D_MODEL, 2 * D_FF), D_MODEL ** -0.5),
        'conv_w': nrm(ks[16], (DEPTH, CONV_WIDTH, D_FF), CONV_WIDTH ** -0.5),
        'conv_b': nrm(ks[17], (DEPTH, D_FF), 0.02),
        'w_down': nrm(ks[18], (DEPTH, D_FF, D_MODEL), D_FF ** -0.5),
        'ln_f': gain(ks[19], (D_MODEL,)),
    }


def reference(x_prompt, x_sample, state_ret, state_gla, cache_conv, ln1, w_in, w_gate_up,
              b_gate_up, g_ret, g_gla, w_ret_out, w_gla_out, w_o, ln2, w_up, conv_w, conv_b,
              w_down, ln_f):
    Bp, Lp, _ = x_prompt.shape
    Ls = x_sample.shape[1]
    dt = x_prompt.dtype
    pos_p = jnp.arange(Lp, dtype=jnp.float32)
    pos_s = PAST_LEN + jnp.arange(Ls, dtype=jnp.float32)

    hp, hs = x_prompt, x_sample
    rp, rs, gp, gs, cp, cs = [], [], [], [], [], []
    for l in range(DEPTH):
        wl = (ln1[l], w_in[l], w_gate_up[l], b_gate_up[l], g_ret[l], g_gla[l], w_ret_out[l],
              w_gla_out[l], w_o[l], ln2[l], w_up[l], conv_w[l], conv_b[l], w_down[l])
        zr = jnp.zeros((Bp, RET_HEADS, RET_DK, RET_DV), dt)
        zg = jnp.zeros((Bp, GLA_HEADS, GLA_DK, GLA_DV), dt)
        zc = jnp.zeros((Bp, CONV_WIDTH - 1, D_FF), dt)
        hp, sr_p, sg_p, cv_p = layer(hp, pos_p, zr, zg, zc, *wl)
        hs, sr_s, sg_s, cv_s = layer(hs, pos_s, state_ret[l], state_gla[l], cache_conv[l], *wl)
        rp.append(sr_p); rs.append(sr_s)
        gp.append(sg_p); gs.append(sg_s)
        cp.append(cv_p); cs.append(cv_s)

    y_prompt = rmsnorm(hp, ln_f)
    y_sample = rmsnorm(hs, ln_f)
    return (y_prompt, y_sample, jnp.stack(rp), jnp.stack(rs), jnp.stack(gp), jnp.stack(gs),
            jnp.stack(cp), jnp.stack(cs))
```

```python
import functools
import math

import jax
import jax.numpy as jnp
import numpy as np
from jax import lax
from jax.experimental import pallas as pl
from jax.experimental.pallas import tpu as pltpu

D_MODEL = 1024
PAST_LEN = 16384
RET_HEADS = 4
RET_DK = 128
RET_DV = 128
GLA_HEADS = 4
GLA_DK = 64
GLA_DV = 128
GLA_GATE_RANK = 16
GLA_GATE_TAU = 16.0
D_FF = 2816
CONV_WIDTH = 3
CHUNK = 64
ROPE_BASE = 10000.0
EPS = 1e-6

LANES = 128
SUBLANES = 8
VMEM_LIMIT = 56 * 1024 * 1024

RET_W = RET_HEADS * RET_DK
GLA_QK_W = GLA_HEADS * GLA_DK
GLA_V_W = GLA_HEADS * GLA_DV
SCAN_W = 3 * RET_W + 2 * GLA_QK_W + GLA_V_W
GATE_W = RET_W + GLA_V_W + 2 * D_MODEL
PROJ_W = SCAN_W + GATE_W + LANES
LOG_GAMMA = tuple(math.log(1.0 - 2.0 ** (-5.0 - h)) for h in range(RET_HEADS))

BF16 = jnp.bfloat16
F32 = jnp.float32


def _dot(a, b):
    return jnp.dot(a.astype(BF16), b.astype(BF16), preferred_element_type=F32)


def _dot_nt(a, b):
    return lax.dot_general(a.astype(BF16), b.astype(BF16), (((1,), (1,)), ((), ())),
                           preferred_element_type=F32)


def _dot_tn(a, b):
    return lax.dot_general(a.astype(BF16), b.astype(BF16), (((0,), (0,)), ((), ())),
                           preferred_element_type=F32)


def _split3(x):
    hi = x.astype(BF16)
    r = x - hi.astype(F32)
    mid = r.astype(BF16)
    lo = (r - mid.astype(F32)).astype(BF16)
    return hi, mid, lo


def _dot_exact_lhs01(m01, x):
    hi, mid, lo = _split3(x)
    m = m01.astype(BF16)
    f = lambda t: jnp.dot(m, t, preferred_element_type=F32)
    return f(hi) + f(mid) + f(lo)


def _colsum_bcast(x, width):
    hi, mid, lo = _split3(x)
    ones = jnp.ones((x.shape[0], width), BF16)
    f = lambda t: lax.dot_general(t, ones, (((0,), (0,)), ((), ())), preferred_element_type=F32)
    return f(hi) + f(mid) + f(lo)


def _rms(x, g):
    return x * lax.rsqrt(jnp.mean(x * x, axis=-1, keepdims=True) + EPS) * g


def _sigmoid(x):
    return 1.0 / (1.0 + jnp.exp(-x))


def _proj_kernel(x_ref, cos_ref, sin_ref, ln1_ref, w_ref, wgu_ref, bgu_ref,
                 scan_ref, gate_ref, la_ref):
    x = x_ref[...]
    n = _rms(x, ln1_ref[...]).astype(BF16)
    cos = cos_ref[...]
    sin = sin_ref[...]

    for g in range(2 * RET_HEADS):
        lo = g * RET_DK
        t = jnp.dot(n, w_ref[:, lo:lo + RET_DK], preferred_element_type=F32)
        r = t * cos + pltpu.roll(t, RET_DK // 2, axis=1) * sin
        if g >= RET_HEADS:
            r = r * (RET_DK ** -0.5)
        scan_ref[:, lo:lo + RET_DK] = r
    scan_ref[:, 2 * RET_W:3 * RET_W] = jnp.dot(n, w_ref[:, 2 * RET_W:3 * RET_W],
                                               preferred_element_type=F32)
    o = 3 * RET_W
    scan_ref[:, o:o + GLA_QK_W] = jnp.dot(n, w_ref[:, o:o + GLA_QK_W],
                                          preferred_element_type=F32) * (GLA_DK ** -0.5)
    o += GLA_QK_W
    scan_ref[:, o:SCAN_W] = jnp.dot(n, w_ref[:, o:SCAN_W], preferred_element_type=F32)
    gate_ref[...] = jnp.dot(n, w_ref[:, SCAN_W:SCAN_W + GATE_W], preferred_element_type=F32)
    ga = jnp.dot(n, w_ref[:, SCAN_W + GATE_W:PROJ_W], preferred_element_type=F32)
    z = _dot(ga, wgu_ref[...]) + bgu_ref[...]
    log_sig = -(jnp.maximum(-z, 0.0) + jnp.log1p(jnp.exp(-jnp.abs(z))))
    la_ref[...] = log_sig / GLA_GATE_TAU


def _scan_rows(qkv, la, sret, sgla, seg):
    R = qkv.shape[0]
    n_seq = R // seg
    row = lax.broadcasted_iota(jnp.int32, (R, 1), 0)
    pos = (row % seg).astype(F32)
    sid = row // seg
    ri = lax.broadcasted_iota(jnp.int32, (R, R), 0)
    ci = lax.broadcasted_iota(jnp.int32, (R, R), 1)
    same = (ri // seg) == (ci // seg)
    causal = same & (ri >= ci)
    diff = (ri - ci).astype(F32)
    seq_mask = [sid == a for a in range(n_seq)]

    def pick(a, t):
        return t if n_seq == 1 else jnp.where(seq_mask[a], t, 0.0)

    outs = []
    new_ret = [[None] * RET_HEADS for _ in range(n_seq)]
    for h in range(RET_HEADS):
        lg = LOG_GAMMA[h]
        q = qkv[:, h * RET_DK:(h + 1) * RET_DK]
        k = qkv[:, RET_W + h * RET_DK:RET_W + (h + 1) * RET_DK]
        v = qkv[:, 2 * RET_W + h * RET_DV:2 * RET_W + (h + 1) * RET_DV]
        dm = jnp.where(causal, jnp.exp(lg * diff), 0.0)
        s = _dot_nt(q, k) * dm
        qd = q * jnp.exp(lg * (pos + 1.0))
        kt = k * jnp.exp(lg * (seg - 1.0 - pos))
        o = _dot(s, v)
        for a in range(n_seq):
            o = o + pick(a, _dot(qd, sret[a][h]))
            new_ret[a][h] = math.exp(lg * seg) * sret[a][h] + _dot_tn(pick(a, kt), v)
        outs.append(o)

    go = 3 * RET_W
    gq = qkv[:, go:go + GLA_QK_W]
    gk = qkv[:, go + GLA_QK_W:go + 2 * GLA_QK_W]
    bc = _dot_exact_lhs01(causal, la)
    bt = _dot_exact_lhs01(same, la)
    qd_all = gq * jnp.exp(bc)
    kd_all = gk * jnp.exp(-bc)
    kt_all = gk * jnp.exp(bt - bc)
    lane = lax.broadcasted_iota(jnp.int32, (1, 2 * GLA_DK), 1)
    krow = lax.broadcasted_iota(jnp.int32, (2 * GLA_DK, 1), 0)
    new_gla = [[None] * (GLA_HEADS // 2) for _ in range(n_seq)]
    for p in range(GLA_HEADS // 2):
        cs = slice(p * 2 * GLA_DK, (p + 1) * 2 * GLA_DK)
        qd, kd, kt, lap = qd_all[:, cs], kd_all[:, cs], kt_all[:, cs], la[:, cs]
        vo = go + 2 * GLA_QK_W + p * 2 * GLA_DV
        vp = qkv[:, vo:vo + 2 * GLA_DV]
        for half in range(2):
            qm = jnp.where((lane // GLA_DK) == half, qd, 0.0)
            s = jnp.where(causal, _dot_nt(qm, kd), 0.0)
            o = _dot(s, vp[:, half * GLA_DV:(half + 1) * GLA_DV])
            for a in range(n_seq):
                o = o + pick(a, _dot(qm, sgla[a][p]))
            outs.append(o)
        for a in range(n_seq):
            upd = _dot_tn(pick(a, kt), vp)
            upd = jnp.where(krow < GLA_DK, upd[:, :GLA_DV], upd[:, GLA_DV:])
            dec = jnp.exp(_colsum_bcast(pick(a, lap), GLA_DV))
            new_gla[a][p] = dec * sgla[a][p] + upd
    return jnp.concatenate(outs, axis=1), new_ret, new_gla


def _scan_prompt_kernel(qkv_ref, la_ref, o_ref, sret_ref, sgla_ref):
    @pl.when(pl.program_id(1) == 0)
    def _():
        sret_ref[...] = jnp.zeros_like(sret_ref)
        sgla_ref[...] = jnp.zeros_like(sgla_ref)

    n_chunks = qkv_ref.shape[0] // CHUNK

    def body(c, carry):
        rows = pl.ds(pl.multiple_of(c * CHUNK, CHUNK), CHUNK)
        sret = [[sret_ref[h] for h in range(RET_HEADS)]]
        sgla = [[sgla_ref[p] for p in range(GLA_HEADS // 2)]]
        o, nr, ng = _scan_rows(qkv_ref[rows, :], la_ref[rows, :], sret, sgla, CHUNK)
        o_ref[rows, :] = o
        for h in range(RET_HEADS):
            sret_ref[h] = nr[0][h]
        for p in range(GLA_HEADS // 2):
            sgla_ref[p] = ng[0][p]
        return carry

    lax.fori_loop(0, n_chunks, body, 0)


def _scan_sample_kernel(seg, qkv_ref, la_ref, sret_in_ref, sgla_in_ref, o_ref, sret_ref, sgla_ref):
    n_seq = SUBLANES // seg
    n_groups = qkv_ref.shape[0] // SUBLANES

    def body(c, carry):
        rows = pl.ds(pl.multiple_of(c * SUBLANES, SUBLANES), SUBLANES)
        sret = [[sret_in_ref[c * n_seq + a, h] for h in range(RET_HEADS)] for a in range(n_seq)]
        sgla = [[sgla_in_ref[c * n_seq + a, p] for p in range(GLA_HEADS // 2)]
                for a in range(n_seq)]
        o, nr, ng = _scan_rows(qkv_ref[rows, :], la_ref[rows, :], sret, sgla, seg)
        o_ref[rows, :] = o
        for a in range(n_seq):
            for h in range(RET_HEADS):
                sret_ref[c * n_seq + a, h] = nr[a][h]
            for p in range(GLA_HEADS // 2):
                sgla_ref[c * n_seq + a, p] = ng[a][p]
        return carry

    lax.fori_loop(0, n_groups, body, 0)


def _post_kernel(slab_mode, o_ref, gate_ref, x_ref, *rest):
    if slab_mode:
        cache0_ref, cache1_ref = rest[:2]
        rest = rest[2:]
    (gn_ref, wro_ref, wgo_ref, wo_ref, ln2_ref, wup_ref, cw_ref, cb_ref, wdn_ref, lnf_ref,
     y_ref, tail_ref, ubuf_ref) = rest
    T = x_ref.shape[0]
    step = pl.program_id(0 if slab_mode else 1)

    gate = gate_ref[...]
    gn = gn_ref[...]
    acts = []
    for h in range(RET_HEADS + GLA_HEADS):
        cs = slice(h * LANES, (h + 1) * LANES)
        oh = o_ref[:, cs]
        yh = oh * lax.rsqrt(jnp.mean(oh * oh, axis=-1, keepdims=True) + EPS) * gn[:, cs]
        gh = gate[:, cs]
        acts.append((yh * (gh * _sigmoid(gh))).astype(BF16))
    y_r = jnp.dot(jnp.concatenate(acts[:RET_HEADS], axis=1), wro_ref[...],
                  preferred_element_type=F32)
    y_g = jnp.dot(jnp.concatenate(acts[RET_HEADS:], axis=1), wgo_ref[...],
                  preferred_element_type=F32)
    mo = RET_W + GLA_V_W
    mix = _sigmoid(gate[:, mo:mo + D_MODEL]) * y_r + _sigmoid(gate[:, mo + D_MODEL:]) * y_g
    h1 = x_ref[...] + _dot(mix, wo_ref[...])

    n2 = _rms(h1, ln2_ref[...]).astype(BF16)
    up = jnp.dot(n2, wup_ref[...], preferred_element_type=F32)
    u = up[:, :D_FF]
    vv = up[:, D_FF:]
    cw = cw_ref[...]
    if slab_mode:
        @pl.when(step == 0)
        def _():
            ubuf_ref[0] = cache0_ref[...]
            ubuf_ref[1] = cache1_ref[...]
        u_m2 = ubuf_ref[0]
        u_m1 = ubuf_ref[1]
        ubuf_ref[0] = u_m1
        ubuf_ref[1] = u
        tail_ref[...] = u
    else:
        @pl.when(step == 0)
        def _():
            ubuf_ref[0:SUBLANES, :] = jnp.zeros((SUBLANES, D_FF), F32)
        ubuf_ref[SUBLANES:SUBLANES + T, :] = u
        u_m2 = ubuf_ref[SUBLANES - 2:SUBLANES - 2 + T, :]
        u_m1 = ubuf_ref[SUBLANES - 1:SUBLANES - 1 + T, :]
        ubuf_ref[0:SUBLANES, :] = ubuf_ref[T:T + SUBLANES, :]
        tail_ref[...] = u[T - (CONV_WIDTH - 1):, :]
    uc = cb_ref[...] + u_m2 * cw[0:1, :]
    uc = uc + u_m1 * cw[1:2, :]
    uc = uc + u * cw[2:3, :]
    act = 0.5 * uc * (1.0 + lax.erf(uc * float(np.float32(np.sqrt(0.5))))) * vv
    h2 = h1 + _dot(act, wdn_ref[...])
    y_ref[...] = _rms(h2, lnf_ref[...])


def _const_spec(shape):
    nd = len(shape)
    return pl.BlockSpec(shape, lambda *_: (0,) * nd, pipeline_mode=pl.Buffered(1))


def _params(sem):
    return pltpu.CompilerParams(dimension_semantics=sem, vmem_limit_bytes=VMEM_LIMIT)


def _rope_tables(pos):
    half = RET_DK // 2
    inv = ROPE_BASE ** (-jnp.arange(half, dtype=F32) / half)
    ang = pos[:, None] * inv[None, :]
    cos, sin = jnp.cos(ang), jnp.sin(ang)
    return jnp.concatenate([cos, cos], axis=-1), jnp.concatenate([-sin, sin], axis=-1)


def _prep_weights(w_in, w_gate_up, b_gate_up, g_ret, g_gla, w_ret_out, w_gla_out, w_o, w_up,
                  w_down):
    cols = np.cumsum((0,) + (RET_W, RET_W, RET_W, RET_W, GLA_QK_W, GLA_QK_W, GLA_V_W, GLA_V_W,
                             GLA_GATE_RANK, D_MODEL, D_MODEL))
    part = lambda i: w_in[:, cols[i]:cols[i + 1]]
    w_proj = jnp.concatenate(
        [part(0), part(1), part(2), part(4), part(5), part(6), part(3), part(7), part(9), part(10),
         jnp.pad(part(8), ((0, 0), (0, LANES - GLA_GATE_RANK)))], axis=1).astype(BF16)
    wgu = jnp.pad(w_gate_up, ((0, LANES - GLA_GATE_RANK), (0, 0))).astype(BF16)
    return dict(
        w_proj=w_proj, wgu=wgu, bgu=b_gate_up[None, :],
        gn=jnp.concatenate([g_ret, g_gla])[None, :],
        wro=w_ret_out.astype(BF16), wgo=w_gla_out.astype(BF16), wo=w_o.astype(BF16),
        wup=w_up.astype(BF16), wdn=w_down.astype(BF16))


def _proj_call(x, cos, sin, ln1, wp, row_spec, cos_spec, grid, out_shape):
    return pl.pallas_call(
        _proj_kernel,
        grid=grid,
        in_specs=[row_spec(D_MODEL), cos_spec, cos_spec, _const_spec((1, D_MODEL)),
                  _const_spec((D_MODEL, PROJ_W)), _const_spec((LANES, GLA_QK_W)),
                  _const_spec((1, GLA_QK_W))],
        out_specs=[row_spec(SCAN_W), row_spec(GATE_W), row_spec(GLA_QK_W)],
        out_shape=[jax.ShapeDtypeStruct(out_shape(w), F32) for w in (SCAN_W, GATE_W, GLA_QK_W)],
        compiler_params=_params(("arbitrary",) * len(grid)),
        name="proj",
    )(x, cos, sin, ln1, wp["w_proj"], wp["wgu"], wp["bgu"])


def _post_call(slab_mode, o, gate, x, cache, wp, ln2, conv_w, conv_b, ln_f, row_spec, tail_spec,
               tail_shape, grid, ubuf_shape):
    weights = [wp["gn"], wp["wro"], wp["wgo"], wp["wo"], ln2, wp["wup"], conv_w, conv_b,
               wp["wdn"], ln_f]
    w_specs = [_const_spec(w.shape) for w in weights]
    if slab_mode:
        cache_specs = [pl.BlockSpec((cache.shape[0], D_FF), lambda j, k=k: (0, k))
                       for k in range(CONV_WIDTH - 1)]
        cache_args = [cache] * (CONV_WIDTH - 1)
    else:
        cache_specs, cache_args = [], []
    return pl.pallas_call(
        functools.partial(_post_kernel, slab_mode),
        grid=grid,
        in_specs=[row_spec(RET_W + GLA_V_W), row_spec(GATE_W), row_spec(D_MODEL)]
        + cache_specs + w_specs,
        out_specs=[row_spec(D_MODEL), tail_spec],
        out_shape=[jax.ShapeDtypeStruct(x.shape, F32), jax.ShapeDtypeStruct(tail_shape, F32)],
        scratch_shapes=[pltpu.VMEM(ubuf_shape, F32)],
        compiler_params=_params(("arbitrary",) * len(grid)),
        name="post",
    )(o, gate, x, *cache_args, *weights)


def kernel(x_prompt, x_sample, state_ret, state_gla, cache_conv, ln1, w_in, w_gate_up, b_gate_up,
           g_ret, g_gla, w_ret_out, w_gla_out, w_o, ln2, w_up, conv_w, conv_b, w_down, ln_f):
    Bp, Lp, _ = x_prompt.shape
    Bs, Ls, _ = x_sample.shape
    assert state_ret.shape[0] == 1, "single layer"
    assert SUBLANES % Ls == 0 and (Bs * Ls) % SUBLANES == 0
    wp = _prep_weights(w_in[0], w_gate_up[0], b_gate_up[0], g_ret[0], g_gla[0], w_ret_out[0],
                       w_gla_out[0], w_o[0], w_up[0], w_down[0])
    ln1_, ln2_, lnf_ = ln1[0][None, :], ln2[0][None, :], ln_f[None, :]
    cw, cb = conv_w[0], conv_b[0][None, :]
    cos_p, sin_p = _rope_tables(jnp.arange(Lp, dtype=F32))
    cos_s, sin_s = _rope_tables(PAST_LEN + jnp.arange(Ls, dtype=F32))

    TP = 256
    grid_p = (Bp, Lp // TP)
    rows_p = lambda w: pl.BlockSpec((None, TP, w), lambda b, t: (b, t, 0))
    scan_p, gate_p, la_p = _proj_call(
        x_prompt, cos_p, sin_p, ln1_, wp, rows_p,
        pl.BlockSpec((TP, RET_DK), lambda b, t: (t, 0)), grid_p, lambda w: (Bp, Lp, w))

    TS = 512
    o_p, sret_p, sgla_p = pl.pallas_call(
        _scan_prompt_kernel,
        grid=(Bp, Lp // TS),
        in_specs=[pl.BlockSpec((None, TS, SCAN_W), lambda b, t: (b, t, 0)),
                  pl.BlockSpec((None, TS, GLA_QK_W), lambda b, t: (b, t, 0))],
        out_specs=[pl.BlockSpec((None, TS, RET_W + GLA_V_W), lambda b, t: (b, t, 0)),
                   pl.BlockSpec((None, RET_HEADS, RET_DK, RET_DV), lambda b, t: (b, 0, 0, 0)),
                   pl.BlockSpec((None, GLA_HEADS // 2, 2 * GLA_DK, GLA_DV),
                                lambda b, t: (b, 0, 0, 0))],
        out_shape=[jax.ShapeDtypeStruct((Bp, Lp, RET_W + GLA_V_W), F32),
                   jax.ShapeDtypeStruct((Bp, RET_HEADS, RET_DK, RET_DV), F32),
                   jax.ShapeDtypeStruct((Bp, GLA_HEADS // 2, 2 * GLA_DK, GLA_DV), F32)],
        compiler_params=_params(("arbitrary", "arbitrary")),
        name="scan_prompt",
    )(scan_p, la_p)

    y_p, tail_p = _post_call(
        False, o_p, gate_p, x_prompt, None, wp, ln2_, cw, cb, lnf_, rows_p,
        pl.BlockSpec((None, CONV_WIDTH - 1, D_FF), lambda b, t: (b, 0, 0)),
        (Bp, CONV_WIDTH - 1, D_FF), grid_p, (TP + SUBLANES, D_FF))

    grid_s = (Ls,)
    rows_s = lambda w: pl.BlockSpec((Bs, w), lambda j: (0, j))
    scan_s, gate_s, la_s = _proj_call(
        x_sample.reshape(Bs, Ls * D_MODEL), cos_s[:, None, :], sin_s[:, None, :], ln1_, wp, rows_s,
        pl.BlockSpec((None, 1, RET_DK), lambda j: (j, 0, 0)), grid_s, lambda w: (Bs, Ls * w))

    n_seq = SUBLANES // Ls
    SB = 16
    sgla_in = state_gla[0].reshape(Bs, GLA_HEADS // 2, 2 * GLA_DK, GLA_DV)
    o_s, sret_s, sgla_s = pl.pallas_call(
        functools.partial(_scan_sample_kernel, Ls),
        grid=(Bs // SB,),
        in_specs=[pl.BlockSpec((SB * Ls, SCAN_W), lambda i: (i, 0)),
                  pl.BlockSpec((SB * Ls, GLA_QK_W), lambda i: (i, 0)),
                  pl.BlockSpec((SB, RET_HEADS, RET_DK, RET_DV), lambda i: (i, 0, 0, 0)),
                  pl.BlockSpec((SB, GLA_HEADS // 2, 2 * GLA_DK, GLA_DV), lambda i: (i, 0, 0, 0))],
        out_specs=[pl.BlockSpec((SB * Ls, RET_W + GLA_V_W), lambda i: (i, 0)),
                   pl.BlockSpec((SB, RET_HEADS, RET_DK, RET_DV), lambda i: (i, 0, 0, 0)),
                   pl.BlockSpec((SB, GLA_HEADS // 2, 2 * GLA_DK, GLA_DV), lambda i: (i, 0, 0, 0))],
        out_shape=[jax.ShapeDtypeStruct((Bs * Ls, RET_W + GLA_V_W), F32),
                   jax.ShapeDtypeStruct((Bs, RET_HEADS, RET_DK, RET_DV), F32),
                   jax.ShapeDtypeStruct((Bs, GLA_HEADS // 2, 2 * GLA_DK, GLA_DV), F32)],
        compiler_params=_params(("arbitrary",)),
        name="scan_sample",
    )(scan_s.reshape(Bs * Ls, SCAN_W), la_s.reshape(Bs * Ls, GLA_QK_W), state_ret[0], sgla_in)

    y_s, tail_s = _post_call(
        True, o_s.reshape(Bs, Ls * (RET_W + GLA_V_W)), gate_s, x_sample.reshape(Bs, Ls * D_MODEL),
        cache_conv[0].reshape(Bs, (CONV_WIDTH - 1) * D_FF), wp, ln2_, cw, cb, lnf_, rows_s,
        pl.BlockSpec((Bs, D_FF), lambda j: (0, jnp.maximum(j - (Ls - CONV_WIDTH + 1), 0))),
        (Bs, (CONV_WIDTH - 1) * D_FF), grid_s, (CONV_WIDTH - 1, Bs, D_FF))

    gshape = (1, -1, GLA_HEADS, GLA_DK, GLA_DV)
    return (y_p, y_s.reshape(Bs, Ls, D_MODEL), sret_p[None], sret_s[None],
            sgla_p.reshape(gshape), sgla_s.reshape(gshape),
            tail_p[None], tail_s.reshape(1, Bs, CONV_WIDTH - 1, D_FF))
```

```python
import functools
import math

import jax
import jax.numpy as jnp
import numpy as np
from jax import lax
from jax.experimental import pallas as pl
from jax.experimental.pallas import tpu as pltpu

D_MODEL = 1024
PAST_LEN = 16384
RET_HEADS = 4
RET_DK = 128
RET_DV = 128
GLA_HEADS = 4
GLA_DK = 64
GLA_DV = 128
GLA_GATE_RANK = 16
GLA_GATE_TAU = 16.0
D_FF = 2816
CONV_WIDTH = 3
CHUNK = 64
ROPE_BASE = 10000.0
EPS = 1e-6

LANES = 128
SUBLANES = 8
VMEM_LIMIT = 56 * 1024 * 1024

RET_W = RET_HEADS * RET_DK
GLA_QK_W = GLA_HEADS * GLA_DK
GLA_V_W = GLA_HEADS * GLA_DV
SCAN_W = 3 * RET_W + 2 * GLA_QK_W + GLA_V_W
GATE_W = RET_W + GLA_V_W + 2 * D_MODEL
PROJ_W = SCAN_W + GATE_W + LANES
FFN_BLOCK = 256
LOG_GAMMA = tuple(math.log(1.0 - 2.0 ** (-5.0 - h)) for h in range(RET_HEADS))

BF16 = jnp.bfloat16
F32 = jnp.float32


def _dot(a, b):
    return jnp.dot(a.astype(BF16), b.astype(BF16), preferred_element_type=F32)


def _dot_nt(a, b):
    return lax.dot_general(a.astype(BF16), b.astype(BF16), (((1,), (1,)), ((), ())),
                           preferred_element_type=F32)


def _dot_tn(a, b):
    return lax.dot_general(a.astype(BF16), b.astype(BF16), (((0,), (0,)), ((), ())),
                           preferred_element_type=F32)


def _split3(x):
    hi = x.astype(BF16)
    r = x - hi.astype(F32)
    mid = r.astype(BF16)
    lo = (r - mid.astype(F32)).astype(BF16)
    return hi, mid, lo


def _dot_exact_lhs01(m01, x):
    hi, mid, lo = _split3(x)
    m = m01.astype(BF16)
    f = lambda t: jnp.dot(m, t, preferred_element_type=F32)
    return f(hi) + f(mid) + f(lo)


def _colsum_bcast(x, width):
    hi, mid, lo = _split3(x)
    ones = jnp.ones((x.shape[0], width), BF16)
    f = lambda t: lax.dot_general(t, ones, (((0,), (0,)), ((), ())), preferred_element_type=F32)
    return f(hi) + f(mid) + f(lo)


def _rms(x, g):
    return x * lax.rsqrt(jnp.mean(x * x, axis=-1, keepdims=True) + EPS) * g


def _sigmoid(x):
    return 1.0 / (1.0 + jnp.exp(-x))


def _proj_kernel(x_ref, cos_ref, sin_ref, ln1_ref, w_ref, wgu_ref, bgu_ref,
                 scan_ref, gate_ref, la_ref):
    x = x_ref[...]
    n = _rms(x, ln1_ref[...]).astype(BF16)
    cos = cos_ref[...]
    sin = sin_ref[...]

    qk = jnp.dot(n, w_ref[:, :2 * RET_W], preferred_element_type=F32)
    ga = jnp.dot(n, w_ref[:, SCAN_W + GATE_W:PROJ_W], preferred_element_type=F32)
    z = _dot(ga, wgu_ref[...]) + bgu_ref[...]
    log_sig = -(jnp.maximum(-z, 0.0) + jnp.log1p(jnp.exp(-jnp.abs(z))))
    la_ref[...] = log_sig / GLA_GATE_TAU
    o = 3 * RET_W
    scan_ref[:, 2 * RET_W:o] = jnp.dot(n, w_ref[:, 2 * RET_W:o], preferred_element_type=F32)
    scan_ref[:, o:o + GLA_QK_W] = jnp.dot(n, w_ref[:, o:o + GLA_QK_W],
                                          preferred_element_type=F32) * (GLA_DK ** -0.5)
    o += GLA_QK_W
    scan_ref[:, o:SCAN_W] = jnp.dot(n, w_ref[:, o:SCAN_W], preferred_element_type=F32)
    gate_ref[...] = jnp.dot(n, w_ref[:, SCAN_W:SCAN_W + GATE_W], preferred_element_type=F32)
    for g in range(2 * RET_HEADS):
        lo = g * RET_DK
        t = qk[:, lo:lo + RET_DK]
        r = t * cos + pltpu.roll(t, RET_DK // 2, axis=1) * sin
        if g >= RET_HEADS:
            r = r * (RET_DK ** -0.5)
        scan_ref[:, lo:lo + RET_DK] = r


def _scan_rows(qkv, la, sret, sgla, seg):
    R = qkv.shape[0]
    n_seq = R // seg
    row = lax.broadcasted_iota(jnp.int32, (R, 1), 0)
    pos = (row % seg).astype(F32)
    sid = row // seg
    ri = lax.broadcasted_iota(jnp.int32, (R, R), 0)
    ci = lax.broadcasted_iota(jnp.int32, (R, R), 1)
    same = (ri // seg) == (ci // seg)
    causal = same & (ri >= ci)
    diff = (ri - ci).astype(F32)
    seq_mask = [sid == a for a in range(n_seq)]

    def pick(a, t):
        return t if n_seq == 1 else jnp.where(seq_mask[a], t, 0.0)

    outs = []
    new_ret = [[None] * RET_HEADS for _ in range(n_seq)]
    for h in range(RET_HEADS):
        lg = LOG_GAMMA[h]
        q = qkv[:, h * RET_DK:(h + 1) * RET_DK]
        k = qkv[:, RET_W + h * RET_DK:RET_W + (h + 1) * RET_DK]
        v = qkv[:, 2 * RET_W + h * RET_DV:2 * RET_W + (h + 1) * RET_DV]
        dm = jnp.where(causal, jnp.exp(lg * diff), 0.0)
        s = _dot_nt(q, k) * dm
        qd = q * jnp.exp(lg * (pos + 1.0))
        kt = k * jnp.exp(lg * (seg - 1.0 - pos))
        o = _dot(s, v)
        for a in range(n_seq):
            o = o + pick(a, _dot(qd, sret[a][h]))
            new_ret[a][h] = math.exp(lg * seg) * sret[a][h] + _dot_tn(pick(a, kt), v)
        outs.append(o)

    go = 3 * RET_W
    gq = qkv[:, go:go + GLA_QK_W]
    gk = qkv[:, go + GLA_QK_W:go + 2 * GLA_QK_W]
    bc = _dot_exact_lhs01(causal, la)
    bt = _dot_exact_lhs01(same, la)
    qd_all = gq * jnp.exp(bc)
    kd_all = gk * jnp.exp(-bc)
    kt_all = gk * jnp.exp(bt - bc)
    lane = lax.broadcasted_iota(jnp.int32, (1, 2 * GLA_DK), 1)
    krow = lax.broadcasted_iota(jnp.int32, (2 * GLA_DK, 1), 0)
    new_gla = [[None] * (GLA_HEADS // 2) for _ in range(n_seq)]
    for p in range(GLA_HEADS // 2):
        cs = slice(p * 2 * GLA_DK, (p + 1) * 2 * GLA_DK)
        qd, kd, kt, lap = qd_all[:, cs], kd_all[:, cs], kt_all[:, cs], la[:, cs]
        vo = go + 2 * GLA_QK_W + p * 2 * GLA_DV
        vp = qkv[:, vo:vo + 2 * GLA_DV]
        for half in range(2):
            qm = jnp.where((lane // GLA_DK) == half, qd, 0.0)
            s = jnp.where(causal, _dot_nt(qm, kd), 0.0)
            o = _dot(s, vp[:, half * GLA_DV:(half + 1) * GLA_DV])
            for a in range(n_seq):
                o = o + pick(a, _dot(qm, sgla[a][p]))
            outs.append(o)
        for a in range(n_seq):
            upd = _dot_tn(pick(a, kt), vp)
            upd = jnp.where(krow < GLA_DK, upd[:, :GLA_DV], upd[:, GLA_DV:])
            dec = jnp.exp(_colsum_bcast(pick(a, lap), GLA_DV))
            new_gla[a][p] = dec * sgla[a][p] + upd
    return jnp.concatenate(outs, axis=1), new_ret, new_gla


def _scan_prompt_kernel(qkv_ref, la_ref, o_ref, sret_ref, sgla_ref):
    @pl.when(pl.program_id(1) == 0)
    def _():
        sret_ref[...] = jnp.zeros_like(sret_ref)
        sgla_ref[...] = jnp.zeros_like(sgla_ref)

    n_chunks = qkv_ref.shape[0] // CHUNK

    def body(c, carry):
        rows = pl.ds(pl.multiple_of(c * CHUNK, CHUNK), CHUNK)
        sret = [[sret_ref[h] for h in range(RET_HEADS)]]
        sgla = [[sgla_ref[p] for p in range(GLA_HEADS // 2)]]
        o, nr, ng = _scan_rows(qkv_ref[rows, :], la_ref[rows, :], sret, sgla, CHUNK)
        o_ref[rows, :] = o
        for h in range(RET_HEADS):
            sret_ref[h] = nr[0][h]
        for p in range(GLA_HEADS // 2):
            sgla_ref[p] = ng[0][p]
        return carry

    lax.fori_loop(0, n_chunks, body, 0)


def _scan_sample_kernel(seg, qkv_ref, la_ref, sret_in_ref, sgla_in_ref, o_ref, sret_ref, sgla_ref):
    n_seq = SUBLANES // seg
    n_groups = qkv_ref.shape[0] // SUBLANES

    def body(c, carry):
        rows = pl.ds(pl.multiple_of(c * SUBLANES, SUBLANES), SUBLANES)
        sret = [[sret_in_ref[c * n_seq + a, h] for h in range(RET_HEADS)] for a in range(n_seq)]
        sgla = [[sgla_in_ref[c * n_seq + a, p] for p in range(GLA_HEADS // 2)]
                for a in range(n_seq)]
        o, nr, ng = _scan_rows(qkv_ref[rows, :], la_ref[rows, :], sret, sgla, seg)
        o_ref[rows, :] = o
        for a in range(n_seq):
            for h in range(RET_HEADS):
                sret_ref[c * n_seq + a, h] = nr[a][h]
            for p in range(GLA_HEADS // 2):
                sgla_ref[c * n_seq + a, p] = ng[a][p]
        return carry

    lax.fori_loop(0, n_groups, body, 0)


def _post_kernel(slab_mode, o_ref, gate_ref, x_ref, *rest):
    if slab_mode:
        cache0_ref, cache1_ref = rest[:2]
        rest = rest[2:]
    (gn_ref, wro_ref, wgo_ref, wo_ref, ln2_ref, wup_ref, cw_ref, cb_ref, wdn_ref, lnf_ref,
     y_ref, tail_ref, ubuf_ref) = rest
    T = x_ref.shape[0]
    step = pl.program_id(0 if slab_mode else 1)

    gate = gate_ref[...]
    gn = gn_ref[...]
    acts = []
    for h in range(RET_HEADS + GLA_HEADS):
        cs = slice(h * LANES, (h + 1) * LANES)
        oh = o_ref[:, cs]
        yh = oh * lax.rsqrt(jnp.mean(oh * oh, axis=-1, keepdims=True) + EPS) * gn[:, cs]
        gh = gate[:, cs]
        acts.append((yh * (gh * _sigmoid(gh))).astype(BF16))
    y_r = jnp.dot(jnp.concatenate(acts[:RET_HEADS], axis=1), wro_ref[...],
                  preferred_element_type=F32)
    y_g = jnp.dot(jnp.concatenate(acts[RET_HEADS:], axis=1), wgo_ref[...],
                  preferred_element_type=F32)
    mo = RET_W + GLA_V_W
    mix = _sigmoid(gate[:, mo:mo + D_MODEL]) * y_r + _sigmoid(gate[:, mo + D_MODEL:]) * y_g
    h1 = x_ref[...] + _dot(mix, wo_ref[...])

    n2 = _rms(h1, ln2_ref[...]).astype(BF16)
    if slab_mode:
        @pl.when(step == 0)
        def _():
            ubuf_ref[0] = cache0_ref[...]
            ubuf_ref[1] = cache1_ref[...]
    else:
        @pl.when(step == 0)
        def _():
            ubuf_ref[0:SUBLANES, :] = jnp.zeros((SUBLANES, D_FF), F32)

    h2 = h1
    nb = D_FF // FFN_BLOCK
    up_block = lambda j: jnp.dot(n2, wup_ref[:, 2 * j * FFN_BLOCK:2 * (j + 1) * FFN_BLOCK],
                                 preferred_element_type=F32)
    uv_next = up_block(0)
    for j in range(nb):
        cs = slice(j * FFN_BLOCK, (j + 1) * FFN_BLOCK)
        uv = uv_next
        if j + 1 < nb:
            uv_next = up_block(j + 1)
        u = uv[:, :FFN_BLOCK]
        vv = uv[:, FFN_BLOCK:]
        if slab_mode:
            u_m2 = ubuf_ref[0, :, cs]
            u_m1 = ubuf_ref[1, :, cs]
            ubuf_ref[0, :, cs] = u_m1
            ubuf_ref[1, :, cs] = u
            tail_ref[:, cs] = u
        else:
            ubuf_ref[SUBLANES:SUBLANES + T, cs] = u
            u_m2 = ubuf_ref[SUBLANES - 2:SUBLANES - 2 + T, cs]
            u_m1 = ubuf_ref[SUBLANES - 1:SUBLANES - 1 + T, cs]
        uc = cb_ref[:, cs] + u_m2 * cw_ref[0:1, cs]
        uc = uc + u_m1 * cw_ref[1:2, cs]
        uc = uc + u * cw_ref[2:3, cs]
        act = 0.5 * uc * (1.0 + lax.erf(uc * float(np.float32(np.sqrt(0.5))))) * vv
        h2 = h2 + _dot(act, wdn_ref[cs, :])
    if not slab_mode:
        tail_ref[...] = ubuf_ref[T + SUBLANES - (CONV_WIDTH - 1):T + SUBLANES, :]
        ubuf_ref[0:SUBLANES, :] = ubuf_ref[T:T + SUBLANES, :]
    y_ref[...] = _rms(h2, lnf_ref[...])


def _const_spec(shape):
    nd = len(shape)
    return pl.BlockSpec(shape, lambda *_: (0,) * nd, pipeline_mode=pl.Buffered(1))


def _params(sem):
    return pltpu.CompilerParams(dimension_semantics=sem, vmem_limit_bytes=VMEM_LIMIT)


def _rope_tables(pos):
    half = RET_DK // 2
    inv = ROPE_BASE ** (-jnp.arange(half, dtype=F32) / half)
    ang = pos[:, None] * inv[None, :]
    cos, sin = jnp.cos(ang), jnp.sin(ang)
    return jnp.concatenate([cos, cos], axis=-1), jnp.concatenate([-sin, sin], axis=-1)


def _prep_weights(w_in, w_gate_up, b_gate_up, g_ret, g_gla, w_ret_out, w_gla_out, w_o, w_up,
                  w_down):
    cols = np.cumsum((0,) + (RET_W, RET_W, RET_W, RET_W, GLA_QK_W, GLA_QK_W, GLA_V_W, GLA_V_W,
                             GLA_GATE_RANK, D_MODEL, D_MODEL))
    part = lambda i: w_in[:, cols[i]:cols[i + 1]]
    w_proj = jnp.concatenate(
        [part(0), part(1), part(2), part(4), part(5), part(6), part(3), part(7), part(9), part(10),
         jnp.pad(part(8), ((0, 0), (0, LANES - GLA_GATE_RANK)))], axis=1).astype(BF16)
    wgu = jnp.pad(w_gate_up, ((0, LANES - GLA_GATE_RANK), (0, 0))).astype(BF16)
    nb = D_FF // FFN_BLOCK
    w_up = w_up.reshape(D_MODEL, 2, nb, FFN_BLOCK).transpose(0, 2, 1, 3).reshape(D_MODEL, 2 * D_FF)
    return dict(
        w_proj=w_proj, wgu=wgu, bgu=b_gate_up[None, :],
        gn=jnp.concatenate([g_ret, g_gla])[None, :],
        wro=w_ret_out.astype(BF16), wgo=w_gla_out.astype(BF16), wo=w_o.astype(BF16),
        wup=w_up.astype(BF16), wdn=w_down.astype(BF16))


def _proj_call(x, cos, sin, ln1, wp, row_spec, cos_spec, grid, out_shape):
    return pl.pallas_call(
        _proj_kernel,
        grid=grid,
        in_specs=[row_spec(D_MODEL), cos_spec, cos_spec, _const_spec((1, D_MODEL)),
                  _const_spec((D_MODEL, PROJ_W)), _const_spec((LANES, GLA_QK_W)),
                  _const_spec((1, GLA_QK_W))],
        out_specs=[row_spec(SCAN_W), row_spec(GATE_W), row_spec(GLA_QK_W)],
        out_shape=[jax.ShapeDtypeStruct(out_shape(w), F32) for w in (SCAN_W, GATE_W, GLA_QK_W)],
        compiler_params=_params(("arbitrary",) * len(grid)),
        name="proj",
    )(x, cos, sin, ln1, wp["w_proj"], wp["wgu"], wp["bgu"])


def _post_call(slab_mode, o, gate, x, cache, wp, ln2, conv_w, conv_b, ln_f, row_spec, tail_spec,
               tail_shape, grid, ubuf_shape):
    weights = [wp["gn"], wp["wro"], wp["wgo"], wp["wo"], ln2, wp["wup"], conv_w, conv_b,
               wp["wdn"], ln_f]
    w_specs = [_const_spec(w.shape) for w in weights]
    if slab_mode:
        cache_specs = [pl.BlockSpec((cache.shape[0], D_FF), lambda j, k=k: (0, k))
                       for k in range(CONV_WIDTH - 1)]
        cache_args = [cache] * (CONV_WIDTH - 1)
    else:
        cache_specs, cache_args = [], []
    return pl.pallas_call(
        functools.partial(_post_kernel, slab_mode),
        grid=grid,
        in_specs=[row_spec(RET_W + GLA_V_W), row_spec(GATE_W), row_spec(D_MODEL)]
        + cache_specs + w_specs,
        out_specs=[row_spec(D_MODEL), tail_spec],
        out_shape=[jax.ShapeDtypeStruct(x.shape, F32), jax.ShapeDtypeStruct(tail_shape, F32)],
        scratch_shapes=[pltpu.VMEM(ubuf_shape, F32)],
        compiler_params=_params(("arbitrary",) * len(grid)),
        name="post",
    )(o, gate, x, *cache_args, *weights)


def kernel(x_prompt, x_sample, state_ret, state_gla, cache_conv, ln1, w_in, w_gate_up, b_gate_up,
           g_ret, g_gla, w_ret_out, w_gla_out, w_o, ln2, w_up, conv_w, conv_b, w_down, ln_f):
    Bp, Lp, _ = x_prompt.shape
    Bs, Ls, _ = x_sample.shape
    assert state_ret.shape[0] == 1, "single layer"
    assert SUBLANES % Ls == 0 and (Bs * Ls) % SUBLANES == 0
    wp = _prep_weights(w_in[0], w_gate_up[0], b_gate_up[0], g_ret[0], g_gla[0], w_ret_out[0],
                       w_gla_out[0], w_o[0], w_up[0], w_down[0])
    ln1_, ln2_, lnf_ = ln1[0][None, :], ln2[0][None, :], ln_f[None, :]
    cw, cb = conv_w[0], conv_b[0][None, :]
    cos_p, sin_p = _rope_tables(jnp.arange(Lp, dtype=F32))
    cos_s, sin_s = _rope_tables(PAST_LEN + jnp.arange(Ls, dtype=F32))

    TP = 256
    grid_p = (Bp, Lp // TP)
    rows_p = lambda w: pl.BlockSpec((None, TP, w), lambda b, t: (b, t, 0))
    scan_p, gate_p, la_p = _proj_call(
        x_prompt, cos_p, sin_p, ln1_, wp, rows_p,
        pl.BlockSpec((TP, RET_DK), lambda b, t: (t, 0)), grid_p, lambda w: (Bp, Lp, w))

    TS = 512
    o_p, sret_p, sgla_p = pl.pallas_call(
        _scan_prompt_kernel,
        grid=(Bp, Lp // TS),
        in_specs=[pl.BlockSpec((None, TS, SCAN_W), lambda b, t: (b, t, 0)),
                  pl.BlockSpec((None, TS, GLA_QK_W), lambda b, t: (b, t, 0))],
        out_specs=[pl.BlockSpec((None, TS, RET_W + GLA_V_W), lambda b, t: (b, t, 0)),
                   pl.BlockSpec((None, RET_HEADS, RET_DK, RET_DV), lambda b, t: (b, 0, 0, 0)),
                   pl.BlockSpec((None, GLA_HEADS // 2, 2 * GLA_DK, GLA_DV),
                                lambda b, t: (b, 0, 0, 0))],
        out_shape=[jax.ShapeDtypeStruct((Bp, Lp, RET_W + GLA_V_W), F32),
                   jax.ShapeDtypeStruct((Bp, RET_HEADS, RET_DK, RET_DV), F32),
                   jax.ShapeDtypeStruct((Bp, GLA_HEADS // 2, 2 * GLA_DK, GLA_DV), F32)],
        compiler_params=_params(("arbitrary", "arbitrary")),
        name="scan_prompt",
    )(scan_p, la_p)

    y_p, tail_p = _post_call(
        False, o_p, gate_p, x_prompt, None, wp, ln2_, cw, cb, lnf_, rows_p,
        pl.BlockSpec((None, CONV_WIDTH - 1, D_FF), lambda b, t: (b, 0, 0)),
        (Bp, CONV_WIDTH - 1, D_FF), grid_p, (TP + SUBLANES, D_FF))

    grid_s = (Ls,)
    rows_s = lambda w: pl.BlockSpec((Bs, w), lambda j: (0, j))
    scan_s, gate_s, la_s = _proj_call(
        x_sample.reshape(Bs, Ls * D_MODEL), cos_s[:, None, :], sin_s[:, None, :], ln1_, wp, rows_s,
        pl.BlockSpec((None, 1, RET_DK), lambda j: (j, 0, 0)), grid_s, lambda w: (Bs, Ls * w))

    n_seq = SUBLANES // Ls
    SB = 16
    sgla_in = state_gla[0].reshape(Bs, GLA_HEADS // 2, 2 * GLA_DK, GLA_DV)
    o_s, sret_s, sgla_s = pl.pallas_call(
        functools.partial(_scan_sample_kernel, Ls),
        grid=(Bs // SB,),
        in_specs=[pl.BlockSpec((SB * Ls, SCAN_W), lambda i: (i, 0)),
                  pl.BlockSpec((SB * Ls, GLA_QK_W), lambda i: (i, 0)),
                  pl.BlockSpec((SB, RET_HEADS, RET_DK, RET_DV), lambda i: (i, 0, 0, 0)),
                  pl.BlockSpec((SB, GLA_HEADS // 2, 2 * GLA_DK, GLA_DV), lambda i: (i, 0, 0, 0))],
        out_specs=[pl.BlockSpec((SB * Ls, RET_W + GLA_V_W), lambda i: (i, 0)),
                   pl.BlockSpec((SB, RET_HEADS, RET_DK, RET_DV), lambda i: (i, 0, 0, 0)),
                   pl.BlockSpec((SB, GLA_HEADS // 2, 2 * GLA_DK, GLA_DV), lambda i: (i, 0, 0, 0))],
        out_shape=[jax.ShapeDtypeStruct((Bs * Ls, RET_W + GLA_V_W), F32),
                   jax.ShapeDtypeStruct((Bs, RET_HEADS, RET_DK, RET_DV), F32),
                   jax.ShapeDtypeStruct((Bs, GLA_HEADS // 2, 2 * GLA_DK, GLA_DV), F32)],
        compiler_params=_params(("arbitrary",)),
        name="scan_sample",
    )(scan_s.reshape(Bs * Ls, SCAN_W), la_s.reshape(Bs * Ls, GLA_QK_W), state_ret[0], sgla_in)

    y_s, tail_s = _post_call(
        True, o_s.reshape(Bs, Ls * (RET_W + GLA_V_W)), gate_s, x_sample.reshape(Bs, Ls * D_MODEL),
        cache_conv[0].reshape(Bs, (CONV_WIDTH - 1) * D_FF), wp, ln2_, cw, cb, lnf_, rows_s,
        pl.BlockSpec((Bs, D_FF), lambda j: (0, jnp.maximum(j - (Ls - CONV_WIDTH + 1), 0))),
        (Bs, (CONV_WIDTH - 1) * D_FF), grid_s, (CONV_WIDTH - 1, Bs, D_FF))

    gshape = (1, -1, GLA_HEADS, GLA_DK, GLA_DV)
    return (y_p, y_s.reshape(Bs, Ls, D_MODEL), sret_p[None], sret_s[None],
            sgla_p.reshape(gshape), sgla_s.reshape(gshape),
            tail_p[None], tail_s.reshape(1, Bs, CONV_WIDTH - 1, D_FF))
```

```python
import functools
import math

import jax
import jax.numpy as jnp
import numpy as np
from jax import lax
from jax.experimental import pallas as pl
from jax.experimental.pallas import tpu as pltpu

D_MODEL = 1024
PAST_LEN = 16384
RET_HEADS = 4
RET_DK = 128
RET_DV = 128
GLA_HEADS = 4
GLA_DK = 64
GLA_DV = 128
GLA_GATE_RANK = 16
GLA_GATE_TAU = 16.0
D_FF = 2816
CONV_WIDTH = 3
CHUNK = 64
RET_CHUNK = 128
ROPE_BASE = 10000.0
EPS = 1e-6

LANES = 128
SUBLANES = 8
VMEM_LIMIT = 56 * 1024 * 1024

RET_W = RET_HEADS * RET_DK
GLA_QK_W = GLA_HEADS * GLA_DK
GLA_V_W = GLA_HEADS * GLA_DV
SCAN_W = 3 * RET_W + 2 * GLA_QK_W + GLA_V_W
GATE_W = RET_W + GLA_V_W + 2 * D_MODEL
PROJ_W = SCAN_W + GATE_W + LANES
FFN_BLOCK = 256
LOG_GAMMA = tuple(math.log(1.0 - 2.0 ** (-5.0 - h)) for h in range(RET_HEADS))

BF16 = jnp.bfloat16
F32 = jnp.float32


def _dot(a, b):
    return jnp.dot(a.astype(BF16), b.astype(BF16), preferred_element_type=F32)


def _dot_nt(a, b):
    return lax.dot_general(a.astype(BF16), b.astype(BF16), (((1,), (1,)), ((), ())),
                           preferred_element_type=F32)


def _dot_tn(a, b):
    return lax.dot_general(a.astype(BF16), b.astype(BF16), (((0,), (0,)), ((), ())),
                           preferred_element_type=F32)


def _split3(x):
    hi = x.astype(BF16)
    r = x - hi.astype(F32)
    mid = r.astype(BF16)
    lo = (r - mid.astype(F32)).astype(BF16)
    return hi, mid, lo


def _dot_exact_lhs01(m01, x):
    hi, mid, lo = _split3(x)
    m = m01.astype(BF16)
    f = lambda t: jnp.dot(m, t, preferred_element_type=F32)
    return f(hi) + f(mid) + f(lo)


def _colsum_bcast(x, width):
    hi, mid, lo = _split3(x)
    ones = jnp.ones((x.shape[0], width), BF16)
    f = lambda t: lax.dot_general(t, ones, (((0,), (0,)), ((), ())), preferred_element_type=F32)
    return f(hi) + f(mid) + f(lo)


def _rms(x, g):
    return x * lax.rsqrt(jnp.mean(x * x, axis=-1, keepdims=True) + EPS) * g


def _sigmoid(x):
    return 1.0 / (1.0 + jnp.exp(-x))


def _proj_kernel(x_ref, cos_ref, sin_ref, ln1_ref, w_ref, wgu_ref, bgu_ref,
                 scan_ref, gate_ref, la_ref):
    x = x_ref[...]
    n = _rms(x, ln1_ref[...]).astype(BF16)
    cos = cos_ref[...]
    sin = sin_ref[...]

    qk = jnp.dot(n, w_ref[:, :2 * RET_W], preferred_element_type=F32)
    ga = jnp.dot(n, w_ref[:, SCAN_W + GATE_W:PROJ_W], preferred_element_type=F32)
    z = _dot(ga, wgu_ref[...]) + bgu_ref[...]
    log_sig = -(jnp.maximum(-z, 0.0) + jnp.log1p(jnp.exp(-jnp.abs(z))))
    la_ref[...] = log_sig / GLA_GATE_TAU
    o = 3 * RET_W
    scan_ref[:, 2 * RET_W:o] = jnp.dot(n, w_ref[:, 2 * RET_W:o], preferred_element_type=F32)
    scan_ref[:, o:o + GLA_QK_W] = jnp.dot(n, w_ref[:, o:o + GLA_QK_W],
                                          preferred_element_type=F32) * (GLA_DK ** -0.5)
    o += GLA_QK_W
    scan_ref[:, o:SCAN_W] = jnp.dot(n, w_ref[:, o:SCAN_W], preferred_element_type=F32)
    gate_ref[...] = jnp.dot(n, w_ref[:, SCAN_W:SCAN_W + GATE_W], preferred_element_type=F32)
    for g in range(2 * RET_HEADS):
        lo = g * RET_DK
        t = qk[:, lo:lo + RET_DK]
        r = t * cos + pltpu.roll(t, RET_DK // 2, axis=1) * sin
        if g >= RET_HEADS:
            r = r * (RET_DK ** -0.5)
        scan_ref[:, lo:lo + RET_DK] = r


def _scan_rows(qkv, la, sret, sgla, seg):
    R = qkv.shape[0]
    n_seq = R // seg
    row = lax.broadcasted_iota(jnp.int32, (R, 1), 0)
    pos = (row % seg).astype(F32)
    sid = row // seg
    ri = lax.broadcasted_iota(jnp.int32, (R, R), 0)
    ci = lax.broadcasted_iota(jnp.int32, (R, R), 1)
    same = (ri // seg) == (ci // seg)
    causal = same & (ri >= ci)
    diff = (ri - ci).astype(F32)
    seq_mask = [sid == a for a in range(n_seq)]

    def pick(a, t):
        return t if n_seq == 1 else jnp.where(seq_mask[a], t, 0.0)

    outs = []
    new_ret = [[None] * RET_HEADS for _ in range(n_seq)]
    for h in range(RET_HEADS):
        lg = LOG_GAMMA[h]
        q = qkv[:, h * RET_DK:(h + 1) * RET_DK]
        k = qkv[:, RET_W + h * RET_DK:RET_W + (h + 1) * RET_DK]
        v = qkv[:, 2 * RET_W + h * RET_DV:2 * RET_W + (h + 1) * RET_DV]
        dm = jnp.where(causal, jnp.exp(lg * diff), 0.0)
        s = _dot_nt(q, k) * dm
        qd = q * jnp.exp(lg * (pos + 1.0))
        kt = k * jnp.exp(lg * (seg - 1.0 - pos))
        o = _dot(s, v)
        for a in range(n_seq):
            o = o + pick(a, _dot(qd, sret[a][h]))
            new_ret[a][h] = math.exp(lg * seg) * sret[a][h] + _dot_tn(pick(a, kt), v)
        outs.append(o)

    go = 3 * RET_W
    gq = qkv[:, go:go + GLA_QK_W]
    gk = qkv[:, go + GLA_QK_W:go + 2 * GLA_QK_W]
    bc = _dot_exact_lhs01(causal, la)
    bt = _dot_exact_lhs01(same, la)
    qd_all = gq * jnp.exp(bc)
    kd_all = gk * jnp.exp(-bc)
    kt_all = gk * jnp.exp(bt - bc)
    lane = lax.broadcasted_iota(jnp.int32, (1, 2 * GLA_DK), 1)
    krow = lax.broadcasted_iota(jnp.int32, (2 * GLA_DK, 1), 0)
    new_gla = [[None] * (GLA_HEADS // 2) for _ in range(n_seq)]
    for p in range(GLA_HEADS // 2):
        cs = slice(p * 2 * GLA_DK, (p + 1) * 2 * GLA_DK)
        qd, kd, kt, lap = qd_all[:, cs], kd_all[:, cs], kt_all[:, cs], la[:, cs]
        vo = go + 2 * GLA_QK_W + p * 2 * GLA_DV
        vp = qkv[:, vo:vo + 2 * GLA_DV]
        for half in range(2):
            qm = jnp.where((lane // GLA_DK) == half, qd, 0.0)
            s = jnp.where(causal, _dot_nt(qm, kd), 0.0)
            o = _dot(s, vp[:, half * GLA_DV:(half + 1) * GLA_DV])
            for a in range(n_seq):
                o = o + pick(a, _dot(qm, sgla[a][p]))
            outs.append(o)
        for a in range(n_seq):
            upd = _dot_tn(pick(a, kt), vp)
            upd = jnp.where(krow < GLA_DK, upd[:, :GLA_DV], upd[:, GLA_DV:])
            dec = jnp.exp(_colsum_bcast(pick(a, lap), GLA_DV))
            new_gla[a][p] = dec * sgla[a][p] + upd
    return jnp.concatenate(outs, axis=1), new_ret, new_gla


def _scan_prompt_kernel(qkv_ref, la_ref, o_ref, sret_ref, sgla_ref, rhs_ret_ref, rhs_gla_ref):
    @pl.when(pl.program_id(1) == 0)
    def _():
        sret_ref[...] = jnp.zeros_like(sret_ref)
        sgla_ref[...] = jnp.zeros_like(sgla_ref)

    T = qkv_ref.shape[0]
    n_rc = T // RET_CHUNK
    n_gc = T // CHUNK

    ri = lax.broadcasted_iota(jnp.int32, (RET_CHUNK, RET_CHUNK), 0)
    ci = lax.broadcasted_iota(jnp.int32, (RET_CHUNK, RET_CHUNK), 1)
    diff = (ri - ci).astype(F32)
    pos = (lax.broadcasted_iota(jnp.int32, (T, RET_DK), 0) % RET_CHUNK).astype(F32)
    ret_lhs, ret_upd = [], []
    for h in range(RET_HEADS):
        lg = LOG_GAMMA[h]
        q = qkv_ref[:, h * RET_DK:(h + 1) * RET_DK]
        k = qkv_ref[:, RET_W + h * RET_DK:RET_W + (h + 1) * RET_DK]
        vb = qkv_ref[:, 2 * RET_W + h * RET_DV:2 * RET_W + (h + 1) * RET_DV].astype(BF16)
        qb = q.astype(BF16)
        kb = k.astype(BF16)
        qd = (q * jnp.exp(lg * (pos + 1.0))).astype(BF16)
        kt = (k * jnp.exp(lg * (RET_CHUNK - 1.0 - pos))).astype(BF16)
        dm = jnp.where(ri >= ci, jnp.exp(lg * diff), 0.0)
        for c in range(n_rc):
            rc = slice(c * RET_CHUNK, (c + 1) * RET_CHUNK)
            rhs_ret_ref[c, h, RET_DK:, :] = vb[rc]
            ret_upd.append(lax.dot_general(kt[rc], vb[rc], (((0,), (0,)), ((), ())),
                                           preferred_element_type=F32))
            s = lax.dot_general(qb[rc], kb[rc], (((1,), (1,)), ((), ())),
                                preferred_element_type=F32) * dm
            ret_lhs.append(jnp.concatenate([qd[rc], s.astype(BF16)], axis=1))

    ti = lax.broadcasted_iota(jnp.int32, (T, T), 0)
    tj = lax.broadcasted_iota(jnp.int32, (T, T), 1)
    in_chunk_causal = ((ti // CHUNK) == (tj // CHUNK)) & (ti >= tj)
    go = 3 * RET_W
    la = la_ref[...]
    bc = _dot_exact_lhs01(in_chunk_causal, la)
    bt = jnp.concatenate(
        [jnp.broadcast_to(bc[(c + 1) * CHUNK - 1:(c + 1) * CHUNK, :], (CHUNK, GLA_QK_W))
         for c in range(n_gc)], axis=0)
    gk = qkv_ref[:, go + GLA_QK_W:go + 2 * GLA_QK_W]
    qd_all = qkv_ref[:, go:go + GLA_QK_W] * jnp.exp(bc)
    kd_all = (gk * jnp.exp(-bc)).astype(BF16)
    kt_all = (gk * jnp.exp(bt - bc)).astype(BF16)
    sel = (lax.broadcasted_iota(jnp.int32, (T, LANES), 0)
           == lax.broadcasted_iota(jnp.int32, (T, LANES), 1) * CHUNK + (CHUNK - 1))
    hi, mid, lo = _split3(bc)
    tn = lambda t: lax.dot_general(t, sel.astype(BF16), (((0,), (0,)), ((), ())),
                                   preferred_element_type=F32)
    bl = tn(hi) + tn(mid) + tn(lo)
    lane = lax.broadcasted_iota(jnp.int32, (1, 2 * GLA_DK), 1)
    krow = lax.broadcasted_iota(jnp.int32, (2 * GLA_DK, 1), 0)
    si = lax.broadcasted_iota(jnp.int32, (CHUNK, 2 * CHUNK), 0)
    sj = lax.broadcasted_iota(jnp.int32, (CHUNK, 2 * CHUNK), 1)
    half_causal = [(sj // CHUNK == half) & (si >= sj % CHUNK) for half in range(2)]
    gla_lhs, gla_upd, gla_dec = [], [], []
    for p in range(GLA_HEADS // 2):
        cs = slice(p * 2 * GLA_DK, (p + 1) * 2 * GLA_DK)
        vo = go + 2 * GLA_QK_W + p * 2 * GLA_DV
        vp = qkv_ref[:, vo:vo + 2 * GLA_DV].astype(BF16)
        for c in range(n_gc):
            rc = slice(c * CHUNK, (c + 1) * CHUNK)
            rhs_gla_ref[c, p, 2 * GLA_DK:2 * GLA_DK + CHUNK, :] = vp[rc, :GLA_DV]
            rhs_gla_ref[c, p, 2 * GLA_DK + CHUNK:, :] = vp[rc, GLA_DV:]
            upd = lax.dot_general(kt_all[rc, cs], vp[rc], (((0,), (0,)), ((), ())),
                                  preferred_element_type=F32)
            gla_upd.append(jnp.where(krow < GLA_DK, upd[:, :GLA_DV], upd[:, GLA_DV:]))
            gla_dec.append(jnp.exp(jnp.broadcast_to(bl[cs, c:c + 1], (2 * GLA_DK, GLA_DV))))
            kk = jnp.concatenate([kd_all[rc, cs]] * 2, axis=0)
            blocks = []
            for half in range(2):
                qm = jnp.where((lane // GLA_DK) == half, qd_all[rc, cs], 0.0).astype(BF16)
                s = lax.dot_general(qm, kk, (((1,), (1,)), ((), ())),
                                    preferred_element_type=F32)
                s = jnp.where(half_causal[half], s, 0.0).astype(BF16)
                blocks.append(jnp.concatenate([qm, s], axis=1))
            gla_lhs.append(jnp.concatenate(blocks, axis=0))

    for h in range(RET_HEADS):
        S = sret_ref[h]
        for c in range(n_rc):
            rhs_ret_ref[c, h, :RET_DK, :] = S.astype(BF16)
            S = math.exp(LOG_GAMMA[h] * RET_CHUNK) * S + ret_upd[h * n_rc + c]
        sret_ref[h] = S
    for p in range(GLA_HEADS // 2):
        S = sgla_ref[p]
        for c in range(n_gc):
            rhs_gla_ref[c, p, :2 * GLA_DK, :] = S.astype(BF16)
            S = gla_dec[p * n_gc + c] * S + gla_upd[p * n_gc + c]
        sgla_ref[p] = S
    for h in range(RET_HEADS):
        for c in range(n_rc):
            o_ref[c * RET_CHUNK:(c + 1) * RET_CHUNK, h * RET_DV:(h + 1) * RET_DV] = jnp.dot(
                ret_lhs[h * n_rc + c], rhs_ret_ref[c, h], preferred_element_type=F32)
    for p in range(GLA_HEADS // 2):
        for c in range(n_gc):
            o2 = jnp.dot(gla_lhs[p * n_gc + c], rhs_gla_ref[c, p], preferred_element_type=F32)
            for half in range(2):
                co = RET_W + (2 * p + half) * GLA_DV
                o_ref[c * CHUNK:(c + 1) * CHUNK, co:co + GLA_DV] = o2[half * CHUNK:(half + 1) * CHUNK]


def _scan_sample_kernel(seg, qkv_ref, la_ref, sret_in_ref, sgla_in_ref, o_ref, sret_ref, sgla_ref):
    n_seq = SUBLANES // seg
    n_groups = qkv_ref.shape[0] // SUBLANES

    def body(c, carry):
        rows = pl.ds(pl.multiple_of(c * SUBLANES, SUBLANES), SUBLANES)
        sret = [[sret_in_ref[c * n_seq + a, h] for h in range(RET_HEADS)] for a in range(n_seq)]
        sgla = [[sgla_in_ref[c * n_seq + a, p] for p in range(GLA_HEADS // 2)]
                for a in range(n_seq)]
        o, nr, ng = _scan_rows(qkv_ref[rows, :], la_ref[rows, :], sret, sgla, seg)
        o_ref[rows, :] = o
        for a in range(n_seq):
            for h in range(RET_HEADS):
                sret_ref[c * n_seq + a, h] = nr[a][h]
            for p in range(GLA_HEADS // 2):
                sgla_ref[c * n_seq + a, p] = ng[a][p]
        return carry

    lax.fori_loop(0, n_groups, body, 0)


def _post_kernel(slab_mode, o_ref, gate_ref, x_ref, *rest):
    if slab_mode:
        cache0_ref, cache1_ref = rest[:2]
        rest = rest[2:]
    (gn_ref, wro_ref, wgo_ref, wo_ref, ln2_ref, wup_ref, cw_ref, cb_ref, wdn_ref, lnf_ref,
     y_ref, tail_ref, ubuf_ref) = rest
    T = x_ref.shape[0]
    step = pl.program_id(0 if slab_mode else 1)

    gate = gate_ref[...]
    gn = gn_ref[...]
    acts = []
    for h in range(RET_HEADS + GLA_HEADS):
        cs = slice(h * LANES, (h + 1) * LANES)
        oh = o_ref[:, cs]
        yh = oh * lax.rsqrt(jnp.mean(oh * oh, axis=-1, keepdims=True) + EPS) * gn[:, cs]
        gh = gate[:, cs]
        acts.append((yh * (gh * _sigmoid(gh))).astype(BF16))
    y_r = jnp.dot(jnp.concatenate(acts[:RET_HEADS], axis=1), wro_ref[...],
                  preferred_element_type=F32)
    y_g = jnp.dot(jnp.concatenate(acts[RET_HEADS:], axis=1), wgo_ref[...],
                  preferred_element_type=F32)
    mo = RET_W + GLA_V_W
    mix = _sigmoid(gate[:, mo:mo + D_MODEL]) * y_r + _sigmoid(gate[:, mo + D_MODEL:]) * y_g
    h1 = x_ref[...] + _dot(mix, wo_ref[...])

    n2 = _rms(h1, ln2_ref[...]).astype(BF16)
    if slab_mode:
        @pl.when(step == 0)
        def _():
            ubuf_ref[0] = cache0_ref[...]
            ubuf_ref[1] = cache1_ref[...]
    else:
        @pl.when(step == 0)
        def _():
            ubuf_ref[0:SUBLANES, :] = jnp.zeros((SUBLANES, D_FF), F32)

    h2 = h1
    nb = D_FF // FFN_BLOCK

    def up_block(j):
        lo = j * FFN_BLOCK
        return (jnp.dot(n2, wup_ref[:, lo:lo + FFN_BLOCK], preferred_element_type=F32),
                jnp.dot(n2, wup_ref[:, D_FF + lo:D_FF + lo + FFN_BLOCK],
                        preferred_element_type=F32))

    uv_next = up_block(0)
    for j in range(nb):
        cs = slice(j * FFN_BLOCK, (j + 1) * FFN_BLOCK)
        u, vv = uv_next
        if j + 1 < nb:
            uv_next = up_block(j + 1)
        if slab_mode:
            u_m2 = ubuf_ref[0, :, cs]
            u_m1 = ubuf_ref[1, :, cs]
            ubuf_ref[0, :, cs] = u_m1
            ubuf_ref[1, :, cs] = u
            tail_ref[:, cs] = u
        else:
            ubuf_ref[SUBLANES:SUBLANES + T, cs] = u
            u_m2 = ubuf_ref[SUBLANES - 2:SUBLANES - 2 + T, cs]
            u_m1 = ubuf_ref[SUBLANES - 1:SUBLANES - 1 + T, cs]
        uc = cb_ref[:, cs] + u_m2 * cw_ref[0:1, cs]
        uc = uc + u_m1 * cw_ref[1:2, cs]
        uc = uc + u * cw_ref[2:3, cs]
        act = 0.5 * uc * (1.0 + lax.erf(uc * float(np.float32(np.sqrt(0.5))))) * vv
        h2 = h2 + _dot(act, wdn_ref[cs, :])
    if not slab_mode:
        tail_ref[...] = ubuf_ref[T + SUBLANES - (CONV_WIDTH - 1):T + SUBLANES, :]
        ubuf_ref[0:SUBLANES, :] = ubuf_ref[T:T + SUBLANES, :]
    y_ref[...] = _rms(h2, lnf_ref[...])


def _const_spec(shape):
    nd = len(shape)
    return pl.BlockSpec(shape, lambda *_: (0,) * nd, pipeline_mode=pl.Buffered(1))


def _params(sem):
    return pltpu.CompilerParams(dimension_semantics=sem, vmem_limit_bytes=VMEM_LIMIT)


def _rope_tables(pos):
    half = RET_DK // 2
    inv = ROPE_BASE ** (-jnp.arange(half, dtype=F32) / half)
    ang = pos[:, None] * inv[None, :]
    cos, sin = jnp.cos(ang), jnp.sin(ang)
    return jnp.concatenate([cos, cos], axis=-1), jnp.concatenate([-sin, sin], axis=-1)


def _prep_weights(w_in, w_gate_up, b_gate_up, g_ret, g_gla, w_ret_out, w_gla_out, w_o, w_up,
                  w_down):
    cols = np.cumsum((0,) + (RET_W, RET_W, RET_W, RET_W, GLA_QK_W, GLA_QK_W, GLA_V_W, GLA_V_W,
                             GLA_GATE_RANK, D_MODEL, D_MODEL))
    part = lambda i: w_in[:, cols[i]:cols[i + 1]]
    w_proj = jnp.concatenate(
        [part(0), part(1), part(2), part(4), part(5), part(6), part(3), part(7), part(9), part(10),
         jnp.pad(part(8), ((0, 0), (0, LANES - GLA_GATE_RANK)))], axis=1).astype(BF16)
    wgu = jnp.pad(w_gate_up, ((0, LANES - GLA_GATE_RANK), (0, 0))).astype(BF16)
    return dict(
        w_proj=w_proj, wgu=wgu, bgu=b_gate_up[None, :],
        gn=jnp.concatenate([g_ret, g_gla])[None, :],
        wro=w_ret_out.astype(BF16), wgo=w_gla_out.astype(BF16), wo=w_o.astype(BF16),
        wup=w_up.astype(BF16), wdn=w_down.astype(BF16))


def _proj_call(x, cos, sin, ln1, wp, row_spec, cos_spec, grid, out_shape):
    return pl.pallas_call(
        _proj_kernel,
        grid=grid,
        in_specs=[row_spec(D_MODEL), cos_spec, cos_spec, _const_spec((1, D_MODEL)),
                  _const_spec((D_MODEL, PROJ_W)), _const_spec((LANES, GLA_QK_W)),
                  _const_spec((1, GLA_QK_W))],
        out_specs=[row_spec(SCAN_W), row_spec(GATE_W), row_spec(GLA_QK_W)],
        out_shape=[jax.ShapeDtypeStruct(out_shape(w), F32) for w in (SCAN_W, GATE_W, GLA_QK_W)],
        compiler_params=_params(("arbitrary",) * len(grid)),
        name="proj",
    )(x, cos, sin, ln1, wp["w_proj"], wp["wgu"], wp["bgu"])


def _post_call(slab_mode, o, gate, x, cache, wp, ln2, conv_w, conv_b, ln_f, row_spec, tail_spec,
               tail_shape, grid, ubuf_shape):
    weights = [wp["gn"], wp["wro"], wp["wgo"], wp["wo"], ln2, wp["wup"], conv_w, conv_b,
               wp["wdn"], ln_f]
    w_specs = [_const_spec(w.shape) for w in weights]
    if slab_mode:
        cache_specs = [pl.BlockSpec((cache.shape[0], D_FF), lambda j, k=k: (0, k))
                       for k in range(CONV_WIDTH - 1)]
        cache_args = [cache] * (CONV_WIDTH - 1)
    else:
        cache_specs, cache_args = [], []
    return pl.pallas_call(
        functools.partial(_post_kernel, slab_mode),
        grid=grid,
        in_specs=[row_spec(RET_W + GLA_V_W), row_spec(GATE_W), row_spec(D_MODEL)]
        + cache_specs + w_specs,
        out_specs=[row_spec(D_MODEL), tail_spec],
        out_shape=[jax.ShapeDtypeStruct(x.shape, F32), jax.ShapeDtypeStruct(tail_shape, F32)],
        scratch_shapes=[pltpu.VMEM(ubuf_shape, F32)],
        compiler_params=_params(("arbitrary",) * len(grid)),
        name="post",
    )(o, gate, x, *cache_args, *weights)


def kernel(x_prompt, x_sample, state_ret, state_gla, cache_conv, ln1, w_in, w_gate_up, b_gate_up,
           g_ret, g_gla, w_ret_out, w_gla_out, w_o, ln2, w_up, conv_w, conv_b, w_down, ln_f):
    Bp, Lp, _ = x_prompt.shape
    Bs, Ls, _ = x_sample.shape
    assert state_ret.shape[0] == 1, "single layer"
    assert SUBLANES % Ls == 0 and (Bs * Ls) % SUBLANES == 0
    wp = _prep_weights(w_in[0], w_gate_up[0], b_gate_up[0], g_ret[0], g_gla[0], w_ret_out[0],
                       w_gla_out[0], w_o[0], w_up[0], w_down[0])
    ln1_, ln2_, lnf_ = ln1[0][None, :], ln2[0][None, :], ln_f[None, :]
    cw, cb = conv_w[0], conv_b[0][None, :]
    cos_p, sin_p = _rope_tables(jnp.arange(Lp, dtype=F32))
    cos_s, sin_s = _rope_tables(PAST_LEN + jnp.arange(Ls, dtype=F32))

    TP = 256
    grid_p = (Bp, Lp // TP)
    rows_p = lambda w: pl.BlockSpec((None, TP, w), lambda b, t: (b, t, 0))
    scan_p, gate_p, la_p = _proj_call(
        x_prompt, cos_p, sin_p, ln1_, wp, rows_p,
        pl.BlockSpec((TP, RET_DK), lambda b, t: (t, 0)), grid_p, lambda w: (Bp, Lp, w))

    TS = 256
    o_p, sret_p, sgla_p = pl.pallas_call(
        _scan_prompt_kernel,
        grid=(Bp, Lp // TS),
        in_specs=[pl.BlockSpec((None, TS, SCAN_W), lambda b, t: (b, t, 0)),
                  pl.BlockSpec((None, TS, GLA_QK_W), lambda b, t: (b, t, 0))],
        out_specs=[pl.BlockSpec((None, TS, RET_W + GLA_V_W), lambda b, t: (b, t, 0)),
                   pl.BlockSpec((None, RET_HEADS, RET_DK, RET_DV), lambda b, t: (b, 0, 0, 0)),
                   pl.BlockSpec((None, GLA_HEADS // 2, 2 * GLA_DK, GLA_DV),
                                lambda b, t: (b, 0, 0, 0))],
        out_shape=[jax.ShapeDtypeStruct((Bp, Lp, RET_W + GLA_V_W), F32),
                   jax.ShapeDtypeStruct((Bp, RET_HEADS, RET_DK, RET_DV), F32),
                   jax.ShapeDtypeStruct((Bp, GLA_HEADS // 2, 2 * GLA_DK, GLA_DV), F32)],
        scratch_shapes=[
            pltpu.VMEM((TS // RET_CHUNK, RET_HEADS, RET_DK + RET_CHUNK, RET_DV), BF16),
            pltpu.VMEM((TS // CHUNK, GLA_HEADS // 2, 2 * GLA_DK + 2 * CHUNK, GLA_DV), BF16)],
        compiler_params=_params(("arbitrary", "arbitrary")),
        name="scan_prompt",
    )(scan_p, la_p)

    y_p, tail_p = _post_call(
        False, o_p, gate_p, x_prompt, None, wp, ln2_, cw, cb, lnf_, rows_p,
        pl.BlockSpec((None, CONV_WIDTH - 1, D_FF), lambda b, t: (b, 0, 0)),
        (Bp, CONV_WIDTH - 1, D_FF), grid_p, (TP + SUBLANES, D_FF))

    grid_s = (Ls,)
    rows_s = lambda w: pl.BlockSpec((Bs, w), lambda j: (0, j))
    scan_s, gate_s, la_s = _proj_call(
        x_sample.reshape(Bs, Ls * D_MODEL), cos_s[:, None, :], sin_s[:, None, :], ln1_, wp, rows_s,
        pl.BlockSpec((None, 1, RET_DK), lambda j: (j, 0, 0)), grid_s, lambda w: (Bs, Ls * w))

    n_seq = SUBLANES // Ls
    SB = 16
    sgla_in = state_gla[0].reshape(Bs, GLA_HEADS // 2, 2 * GLA_DK, GLA_DV)
    o_s, sret_s, sgla_s = pl.pallas_call(
        functools.partial(_scan_sample_kernel, Ls),
        grid=(Bs // SB,),
        in_specs=[pl.BlockSpec((SB * Ls, SCAN_W), lambda i: (i, 0)),
                  pl.BlockSpec((SB * Ls, GLA_QK_W), lambda i: (i, 0)),
                  pl.BlockSpec((SB, RET_HEADS, RET_DK, RET_DV), lambda i: (i, 0, 0, 0)),
                  pl.BlockSpec((SB, GLA_HEADS // 2, 2 * GLA_DK, GLA_DV), lambda i: (i, 0, 0, 0))],
        out_specs=[pl.BlockSpec((SB * Ls, RET_W + GLA_V_W), lambda i: (i, 0)),
                   pl.BlockSpec((SB, RET_HEADS, RET_DK, RET_DV), lambda i: (i, 0, 0, 0)),
                   pl.BlockSpec((SB, GLA_HEADS // 2, 2 * GLA_DK, GLA_DV), lambda i: (i, 0, 0, 0))],
        out_shape=[jax.ShapeDtypeStruct((Bs * Ls, RET_W + GLA_V_W), F32),
                   jax.ShapeDtypeStruct((Bs, RET_HEADS, RET_DK, RET_DV), F32),
                   jax.ShapeDtypeStruct((Bs, GLA_HEADS // 2, 2 * GLA_DK, GLA_DV), F32)],
        compiler_params=_params(("arbitrary",)),
        name="scan_sample",
    )(scan_s.reshape(Bs * Ls, SCAN_W), la_s.reshape(Bs * Ls, GLA_QK_W), state_ret[0], sgla_in)

    y_s, tail_s = _post_call(
        True, o_s.reshape(Bs, Ls * (RET_W + GLA_V_W)), gate_s, x_sample.reshape(Bs, Ls * D_MODEL),
        cache_conv[0].reshape(Bs, (CONV_WIDTH - 1) * D_FF), wp, ln2_, cw, cb, lnf_, rows_s,
        pl.BlockSpec((Bs, D_FF), lambda j: (0, jnp.maximum(j - (Ls - CONV_WIDTH + 1), 0))),
        (Bs, (CONV_WIDTH - 1) * D_FF), grid_s, (CONV_WIDTH - 1, Bs, D_FF))

    gshape = (1, -1, GLA_HEADS, GLA_DK, GLA_DV)
    return (y_p, y_s.reshape(Bs, Ls, D_MODEL), sret_p[None], sret_s[None],
            sgla_p.reshape(gshape), sgla_s.reshape(gshape),
            tail_p[None], tail_s.reshape(1, Bs, CONV_WIDTH - 1, D_FF))
```

```python
import functools
import math

import jax
import jax.numpy as jnp
import numpy as np
from jax import lax
from jax.experimental import pallas as pl
from jax.experimental.pallas import tpu as pltpu

D_MODEL = 1024
PAST_LEN = 16384
RET_HEADS = 4
RET_DK = 128
RET_DV = 128
GLA_HEADS = 4
GLA_DK = 64
GLA_DV = 128
GLA_GATE_RANK = 16
GLA_GATE_TAU = 16.0
D_FF = 2816
CONV_WIDTH = 3
CHUNK = 64
RET_CHUNK = 128
ROPE_BASE = 10000.0
EPS = 1e-6

LANES = 128
SUBLANES = 8
VMEM_LIMIT = 56 * 1024 * 1024

RET_W = RET_HEADS * RET_DK
GLA_QK_W = GLA_HEADS * GLA_DK
GLA_V_W = GLA_HEADS * GLA_DV
GLA_PAIRS = GLA_HEADS // 2
PAIR_DK = 2 * GLA_DK
SCAN_W = 3 * RET_W + 2 * GLA_QK_W + GLA_V_W
GATE_W = RET_W + GLA_V_W + 2 * D_MODEL
O_W = RET_W + GLA_V_W
FFN_BLOCK = 256
LOG_GAMMA = tuple(math.log(1.0 - 2.0 ** (-5.0 - h)) for h in range(RET_HEADS))
IN_COLS = tuple(np.cumsum((0, RET_W, RET_W, RET_W, RET_W, GLA_QK_W, GLA_QK_W, GLA_V_W, GLA_V_W,
                           GLA_GATE_RANK, D_MODEL, D_MODEL)).tolist())
MAIN_W = IN_COLS[8]

BF16 = jnp.bfloat16
F32 = jnp.float32


def _dot(a, b):
    return jnp.dot(a.astype(BF16), b.astype(BF16), preferred_element_type=F32)


def _dot_nt(a, b):
    return lax.dot_general(a, b, (((1,), (1,)), ((), ())), preferred_element_type=F32)


def _dot_tn(a, b):
    return lax.dot_general(a, b, (((0,), (0,)), ((), ())), preferred_element_type=F32)


def _split3(x):
    hi = x.astype(BF16)
    r = x - hi.astype(F32)
    mid = r.astype(BF16)
    lo = (r - mid.astype(F32)).astype(BF16)
    return hi, mid, lo


def _dot_exact_lhs01(m01, x):
    m = m01.astype(BF16)
    return sum(jnp.dot(m, t, preferred_element_type=F32) for t in _split3(x))


def _dot_tn_exact_rhs01(x, m01):
    m = m01.astype(BF16)
    return sum(_dot_tn(t, m) for t in _split3(x))


def _rms(x, g):
    return x * lax.rsqrt(jnp.mean(x * x, axis=-1, keepdims=True) + EPS) * g


def _sigmoid(x):
    return 1.0 / (1.0 + jnp.exp(-x))


def _proj_rows(x, cos, sin, ln1_ref, wmain_ref, wm_ref, wga_ref, wgu_ref, bgu_ref,
               put_scan, put_gate, put_la):
    n = _rms(x, ln1_ref[...]).astype(BF16)
    mm = lambda ref, lo, hi: jnp.dot(n, ref[:, lo:hi], preferred_element_type=F32)
    c = IN_COLS
    qk = mm(wmain_ref, c[0], c[2])
    ga = jnp.dot(n, wga_ref[...], preferred_element_type=F32)
    z = _dot(ga, wgu_ref[...]) + bgu_ref[...]
    log_sig = -(jnp.maximum(-z, 0.0) + jnp.log1p(jnp.exp(-jnp.abs(z))))
    put_la(0, log_sig / GLA_GATE_TAU)
    put_scan(2 * RET_W, mm(wmain_ref, c[2], c[3]))
    gqk = mm(wmain_ref, c[4], c[6])
    put_scan(3 * RET_W, gqk[:, :GLA_QK_W] * (GLA_DK ** -0.5))
    put_scan(3 * RET_W + GLA_QK_W, gqk[:, GLA_QK_W:])
    put_scan(3 * RET_W + 2 * GLA_QK_W, mm(wmain_ref, c[6], c[7]))
    put_gate(0, mm(wmain_ref, c[3], c[4]))
    put_gate(RET_W, mm(wmain_ref, c[7], c[8]))
    put_gate(O_W, jnp.dot(n, wm_ref[...], preferred_element_type=F32))
    for g in range(2 * RET_HEADS):
        lo = g * RET_DK
        t = qk[:, lo:lo + RET_DK]
        r = t * cos + pltpu.roll(t, RET_DK // 2, axis=1) * sin
        if g >= RET_HEADS:
            r = r * (RET_DK ** -0.5)
        put_scan(lo, r)


def _putter(ref, base=0):
    def put(lo, val):
        ref[:, base + lo:base + lo + val.shape[1]] = val
    return put


def _proj_prompt_kernel(x_ref, cos_ref, sin_ref, *rest):
    w_refs, (scan_ref, gate_ref, la_ref) = rest[:6], rest[6:]
    _proj_rows(x_ref[...], cos_ref[...], sin_ref[...], *w_refs,
               _putter(scan_ref), _putter(gate_ref), _putter(la_ref))


def _proj_sample_kernel(x_ref, cos_ref, sin_ref, *rest):
    w_refs, (scan_ref, gate_ref, la_ref) = rest[:6], rest[6:]
    for j in range(x_ref.shape[1]):
        _proj_rows(x_ref[:, j, :], cos_ref[j:j + 1, :], sin_ref[j:j + 1, :], *w_refs,
                   _putter(scan_ref, j * SCAN_W), _putter(gate_ref, j * GATE_W),
                   _putter(la_ref, j * GLA_QK_W))


def _scan_prompt_kernel(qkv_ref, la_ref, o_ref, sret_ref, sgla_ref, rhs_ret_ref, rhs_gla_ref):
    @pl.when(pl.program_id(1) == 0)
    def _():
        sret_ref[...] = jnp.zeros_like(sret_ref)
        sgla_ref[...] = jnp.zeros_like(sgla_ref)

    T = qkv_ref.shape[0]
    n_rc = T // RET_CHUNK
    n_gc = T // CHUNK

    ri = lax.broadcasted_iota(jnp.int32, (RET_CHUNK, RET_CHUNK), 0)
    ci = lax.broadcasted_iota(jnp.int32, (RET_CHUNK, RET_CHUNK), 1)
    diff = (ri - ci).astype(F32)
    pos = (lax.broadcasted_iota(jnp.int32, (T, RET_DK), 0) % RET_CHUNK).astype(F32)
    ret_lhs, ret_upd = [], []
    for h in range(RET_HEADS):
        lg = LOG_GAMMA[h]
        q = qkv_ref[:, h * RET_DK:(h + 1) * RET_DK]
        k = qkv_ref[:, RET_W + h * RET_DK:RET_W + (h + 1) * RET_DK]
        vb = qkv_ref[:, 2 * RET_W + h * RET_DV:2 * RET_W + (h + 1) * RET_DV].astype(BF16)
        qb = q.astype(BF16)
        kb = k.astype(BF16)
        qd = (q * jnp.exp(lg * (pos + 1.0))).astype(BF16)
        kt = (k * jnp.exp(lg * (RET_CHUNK - 1.0 - pos))).astype(BF16)
        dm = jnp.where(ri >= ci, jnp.exp(lg * diff), 0.0)
        for c in range(n_rc):
            rc = slice(c * RET_CHUNK, (c + 1) * RET_CHUNK)
            rhs_ret_ref[c, h, RET_DK:, :] = vb[rc]
            ret_upd.append(_dot_tn(kt[rc], vb[rc]))
            s = _dot_nt(qb[rc], kb[rc]) * dm
            ret_lhs.append(jnp.concatenate([qd[rc], s.astype(BF16)], axis=1))

    ti = lax.broadcasted_iota(jnp.int32, (T, T), 0)
    tj = lax.broadcasted_iota(jnp.int32, (T, T), 1)
    in_chunk_causal = ((ti // CHUNK) == (tj // CHUNK)) & (ti >= tj)
    go = 3 * RET_W
    bc = _dot_exact_lhs01(in_chunk_causal, la_ref[...])
    bt = jnp.concatenate(
        [jnp.broadcast_to(bc[(c + 1) * CHUNK - 1:(c + 1) * CHUNK, :], (CHUNK, GLA_QK_W))
         for c in range(n_gc)], axis=0)
    gk = qkv_ref[:, go + GLA_QK_W:go + 2 * GLA_QK_W]
    qd_all = qkv_ref[:, go:go + GLA_QK_W] * jnp.exp(bc)
    kd_all = (gk * jnp.exp(-bc)).astype(BF16)
    kt_all = (gk * jnp.exp(bt - bc)).astype(BF16)
    sel = (lax.broadcasted_iota(jnp.int32, (T, LANES), 0)
           == lax.broadcasted_iota(jnp.int32, (T, LANES), 1) * CHUNK + (CHUNK - 1))
    bl = _dot_tn_exact_rhs01(bc, sel)
    lane = lax.broadcasted_iota(jnp.int32, (1, PAIR_DK), 1)
    krow = lax.broadcasted_iota(jnp.int32, (PAIR_DK, 1), 0)
    si = lax.broadcasted_iota(jnp.int32, (CHUNK, 2 * CHUNK), 0)
    sj = lax.broadcasted_iota(jnp.int32, (CHUNK, 2 * CHUNK), 1)
    half_causal = [(sj // CHUNK == half) & (si >= sj % CHUNK) for half in range(2)]
    gla_lhs, gla_upd, gla_dec = [], [], []
    for p in range(GLA_PAIRS):
        cs = slice(p * PAIR_DK, (p + 1) * PAIR_DK)
        vo = go + 2 * GLA_QK_W + p * 2 * GLA_DV
        vp = qkv_ref[:, vo:vo + 2 * GLA_DV].astype(BF16)
        for c in range(n_gc):
            rc = slice(c * CHUNK, (c + 1) * CHUNK)
            rhs_gla_ref[c, p, PAIR_DK:PAIR_DK + CHUNK, :] = vp[rc, :GLA_DV]
            rhs_gla_ref[c, p, PAIR_DK + CHUNK:, :] = vp[rc, GLA_DV:]
            upd = _dot_tn(kt_all[rc, cs], vp[rc])
            gla_upd.append(jnp.where(krow < GLA_DK, upd[:, :GLA_DV], upd[:, GLA_DV:]))
            gla_dec.append(jnp.exp(jnp.broadcast_to(bl[cs, c:c + 1], (PAIR_DK, GLA_DV))))
            kk = jnp.concatenate([kd_all[rc, cs]] * 2, axis=0)
            blocks = []
            for half in range(2):
                qm = jnp.where((lane // GLA_DK) == half, qd_all[rc, cs], 0.0).astype(BF16)
                s = jnp.where(half_causal[half], _dot_nt(qm, kk), 0.0)
                blocks.append(jnp.concatenate([qm, s.astype(BF16)], axis=1))
            gla_lhs.append(jnp.concatenate(blocks, axis=0))

    for h in range(RET_HEADS):
        S = sret_ref[h]
        for c in range(n_rc):
            rhs_ret_ref[c, h, :RET_DK, :] = S.astype(BF16)
            S = math.exp(LOG_GAMMA[h] * RET_CHUNK) * S + ret_upd[h * n_rc + c]
        sret_ref[h] = S
    for p in range(GLA_PAIRS):
        S = sgla_ref[p]
        for c in range(n_gc):
            rhs_gla_ref[c, p, :PAIR_DK, :] = S.astype(BF16)
            S = gla_dec[p * n_gc + c] * S + gla_upd[p * n_gc + c]
        sgla_ref[p] = S
    for h in range(RET_HEADS):
        for c in range(n_rc):
            o_ref[c * RET_CHUNK:(c + 1) * RET_CHUNK, h * RET_DV:(h + 1) * RET_DV] = jnp.dot(
                ret_lhs[h * n_rc + c], rhs_ret_ref[c, h], preferred_element_type=F32)
    for p in range(GLA_PAIRS):
        for c in range(n_gc):
            o2 = jnp.dot(gla_lhs[p * n_gc + c], rhs_gla_ref[c, p], preferred_element_type=F32)
            for half in range(2):
                co = RET_W + (2 * p + half) * GLA_DV
                o_ref[c * CHUNK:(c + 1) * CHUNK, co:co + GLA_DV] = o2[half * CHUNK:(half + 1) * CHUNK]


def _scan_sample_kernel(n_pos, qkv_ref, la_ref, sret_in_ref, sgla_in_ref, o_ref, sret_ref, sgla_ref):
    n_groups = qkv_ref.shape[0] // SUBLANES
    R = n_pos * SUBLANES
    seq_of_row = lax.broadcasted_iota(jnp.int32, (R, 1), 0) % SUBLANES
    seq_of_row2 = lax.broadcasted_iota(jnp.int32, (2 * R, 1), 0) % SUBLANES
    lane = lax.broadcasted_iota(jnp.int32, (1, PAIR_DK), 1)
    krow = lax.broadcasted_iota(jnp.int32, (PAIR_DK, 1), 0)
    eye = (lax.broadcasted_iota(jnp.int32, (SUBLANES, LANES), 0)
           == lax.broadcasted_iota(jnp.int32, (SUBLANES, LANES), 1))
    stack = lambda ts: jnp.concatenate(ts, axis=0)

    def body(g, carry):
        rows = pl.ds(pl.multiple_of(g * SUBLANES, SUBLANES), SUBLANES)
        col = lambda j, lo, w: qkv_ref[rows, j * SCAN_W + lo:j * SCAN_W + lo + w]

        for h in range(RET_HEADS):
            lg = LOG_GAMMA[h]
            q = [col(j, h * RET_DK, RET_DK) for j in range(n_pos)]
            k = [col(j, RET_W + h * RET_DK, RET_DK) for j in range(n_pos)]
            v = [col(j, 2 * RET_W + h * RET_DV, RET_DV) for j in range(n_pos)]
            intra = []
            for i in range(n_pos):
                acc = None
                for j in range(i + 1):
                    s = jnp.sum(q[i] * k[j], axis=-1, keepdims=True) * math.exp(lg * (i - j))
                    acc = s * v[j] if acc is None else acc + s * v[j]
                intra.append(acc)
            qd = stack([q[i] * math.exp(lg * (i + 1)) for i in range(n_pos)]).astype(BF16)
            kt = stack([k[j] * math.exp(lg * (n_pos - 1 - j)) for j in range(n_pos)])
            vs = stack(v).astype(BF16)
            inter = jnp.zeros((R, RET_DV), F32)
            for r in range(SUBLANES):
                mine = seq_of_row == r
                S = sret_in_ref[g * SUBLANES + r, h]
                inter = jnp.where(mine, jnp.dot(qd, S.astype(BF16), preferred_element_type=F32),
                                  inter)
                upd = _dot_tn(jnp.where(mine, kt, 0.0).astype(BF16), vs)
                sret_ref[g * SUBLANES + r, h] = math.exp(lg * n_pos) * S + upd
            for i in range(n_pos):
                o_ref[rows, i * O_W + h * RET_DV:i * O_W + (h + 1) * RET_DV] = (
                    intra[i] + inter[i * SUBLANES:(i + 1) * SUBLANES])

        go = 3 * RET_W
        for p in range(GLA_PAIRS):
            la = [la_ref[rows, j * GLA_QK_W + p * PAIR_DK:j * GLA_QK_W + (p + 1) * PAIR_DK]
                  for j in range(n_pos)]
            bc = [la[0]]
            for j in range(1, n_pos):
                bc.append(bc[-1] + la[j])
            bt = bc[-1]
            gq = [col(j, go + p * PAIR_DK, PAIR_DK) for j in range(n_pos)]
            gk = [col(j, go + GLA_QK_W + p * PAIR_DK, PAIR_DK) for j in range(n_pos)]
            vp = [col(j, go + 2 * GLA_QK_W + p * 2 * GLA_DV, 2 * GLA_DV) for j in range(n_pos)]
            qd = [gq[j] * jnp.exp(bc[j]) for j in range(n_pos)]
            kd = [gk[j] * jnp.exp(-bc[j]) for j in range(n_pos)]
            kt = stack([gk[j] * jnp.exp(bt - bc[j]) for j in range(n_pos)])
            halves = [(lane // GLA_DK) == half for half in range(2)]
            intra = [[None] * n_pos for _ in range(2)]
            for i in range(n_pos):
                for j in range(i + 1):
                    prod = qd[i] * kd[j]
                    for half in range(2):
                        s = jnp.sum(jnp.where(halves[half], prod, 0.0), axis=-1, keepdims=True)
                        term = s * vp[j][:, half * GLA_DV:(half + 1) * GLA_DV]
                        intra[half][i] = term if intra[half][i] is None else intra[half][i] + term
            qs = stack(qd)
            lhs = stack([jnp.where(halves[half], qs, 0.0) for half in range(2)]).astype(BF16)
            vs = stack(vp).astype(BF16)
            btt = _dot_tn_exact_rhs01(bt, eye)
            inter = jnp.zeros((2 * R, GLA_DV), F32)
            for r in range(SUBLANES):
                S = sgla_in_ref[g * SUBLANES + r, p]
                inter = jnp.where(seq_of_row2 == r,
                                  jnp.dot(lhs, S.astype(BF16), preferred_element_type=F32), inter)
                upd = _dot_tn(jnp.where(seq_of_row == r, kt, 0.0).astype(BF16), vs)
                upd = jnp.where(krow < GLA_DK, upd[:, :GLA_DV], upd[:, GLA_DV:])
                dec = jnp.exp(jnp.broadcast_to(btt[:, r:r + 1], (PAIR_DK, GLA_DV)))
                sgla_ref[g * SUBLANES + r, p] = dec * S + upd
            for half in range(2):
                for i in range(n_pos):
                    co = i * O_W + RET_W + (2 * p + half) * GLA_DV
                    lo = half * R + i * SUBLANES
                    o_ref[rows, co:co + GLA_DV] = intra[half][i] + inter[lo:lo + SUBLANES]
        return carry

    lax.fori_loop(0, n_groups, body, 0)


def _post_rows(o, gate, x, gn_ref, wro_ref, wgo_ref, wo_ref, ln2_ref, wup_ref, cw_ref, cb_ref,
               wdn_ref, lnf_ref, conv_taps):
    gn = gn_ref[...]
    acts = []
    for h in range(RET_HEADS + GLA_HEADS):
        cs = slice(h * LANES, (h + 1) * LANES)
        oh = o[:, cs]
        yh = oh * lax.rsqrt(jnp.mean(oh * oh, axis=-1, keepdims=True) + EPS) * gn[:, cs]
        gh = gate[:, cs]
        acts.append((yh * (gh * _sigmoid(gh))).astype(BF16))
    y_r = jnp.dot(jnp.concatenate(acts[:RET_HEADS], axis=1), wro_ref[...],
                  preferred_element_type=F32)
    y_g = jnp.dot(jnp.concatenate(acts[RET_HEADS:], axis=1), wgo_ref[...],
                  preferred_element_type=F32)
    mix = _sigmoid(gate[:, O_W:O_W + D_MODEL]) * y_r + _sigmoid(gate[:, O_W + D_MODEL:]) * y_g
    h1 = x + _dot(mix, wo_ref[...])
    n2 = _rms(h1, ln2_ref[...]).astype(BF16)

    nb = D_FF // FFN_BLOCK

    def up_block(j):
        lo = j * FFN_BLOCK
        return (jnp.dot(n2, wup_ref[:, lo:lo + FFN_BLOCK], preferred_element_type=F32),
                jnp.dot(n2, wup_ref[:, D_FF + lo:D_FF + lo + FFN_BLOCK],
                        preferred_element_type=F32))

    h2 = h1
    uv_next = up_block(0)
    for j in range(nb):
        cs = slice(j * FFN_BLOCK, (j + 1) * FFN_BLOCK)
        u, vv = uv_next
        if j + 1 < nb:
            uv_next = up_block(j + 1)
        u_m2, u_m1 = conv_taps(cs, u)
        uc = cb_ref[:, cs] + u_m2 * cw_ref[0:1, cs]
        uc = uc + u_m1 * cw_ref[1:2, cs]
        uc = uc + u * cw_ref[2:3, cs]
        act = 0.5 * uc * (1.0 + lax.erf(uc * float(np.float32(np.sqrt(0.5))))) * vv
        h2 = h2 + _dot(act, wdn_ref[cs, :])
    return _rms(h2, lnf_ref[...])


def _post_prompt_kernel(o_ref, gate_ref, x_ref, *rest):
    w_refs, (y_ref, tail_ref, ubuf_ref) = rest[:10], rest[10:]
    T = x_ref.shape[0]

    @pl.when(pl.program_id(1) == 0)
    def _():
        ubuf_ref[0:SUBLANES, :] = jnp.zeros((SUBLANES, D_FF), F32)

    def conv_taps(cs, u):
        ubuf_ref[SUBLANES:SUBLANES + T, cs] = u
        return (ubuf_ref[SUBLANES - 2:SUBLANES - 2 + T, cs],
                ubuf_ref[SUBLANES - 1:SUBLANES - 1 + T, cs])

    y_ref[...] = _post_rows(o_ref[...], gate_ref[...], x_ref[...], *w_refs, conv_taps)
    tail_ref[...] = ubuf_ref[T + SUBLANES - (CONV_WIDTH - 1):T + SUBLANES, :]
    ubuf_ref[0:SUBLANES, :] = ubuf_ref[T:T + SUBLANES, :]


def _post_sample_kernel(o_ref, gate_ref, x_ref, cache_ref, *rest):
    w_refs, (y_ref, tail_ref, ubuf_ref) = rest[:10], rest[10:]
    for k in range(CONV_WIDTH - 1):
        ubuf_ref[k] = cache_ref[:, k, :]

    def conv_taps(cs, u):
        u_m2 = ubuf_ref[0, :, cs]
        u_m1 = ubuf_ref[1, :, cs]
        ubuf_ref[0, :, cs] = u_m1
        ubuf_ref[1, :, cs] = u
        return u_m2, u_m1

    for j in range(x_ref.shape[1]):
        y_ref[:, j, :] = _post_rows(o_ref[:, j * O_W:(j + 1) * O_W],
                                    gate_ref[:, j * GATE_W:(j + 1) * GATE_W], x_ref[:, j, :],
                                    *w_refs, conv_taps)
    for k in range(CONV_WIDTH - 1):
        tail_ref[:, k, :] = ubuf_ref[k]


def _const_spec(shape):
    nd = len(shape)
    return pl.BlockSpec(shape, lambda *_: (0,) * nd, pipeline_mode=pl.Buffered(1))


def _params(n_grid):
    return pltpu.CompilerParams(dimension_semantics=("arbitrary",) * n_grid,
                                vmem_limit_bytes=VMEM_LIMIT)


def _rope_tables(pos):
    half = RET_DK // 2
    inv = ROPE_BASE ** (-jnp.arange(half, dtype=F32) / half)
    ang = pos[:, None] * inv[None, :]
    cos, sin = jnp.cos(ang), jnp.sin(ang)
    return jnp.concatenate([cos, cos], axis=-1), jnp.concatenate([-sin, sin], axis=-1)


def kernel(x_prompt, x_sample, state_ret, state_gla, cache_conv, ln1, w_in, w_gate_up, b_gate_up,
           g_ret, g_gla, w_ret_out, w_gla_out, w_o, ln2, w_up, conv_w, conv_b, w_down, ln_f):
    Bp, Lp, _ = x_prompt.shape
    Bs, Ls, _ = x_sample.shape
    assert state_ret.shape[0] == 1, "single layer"
    assert Ls >= CONV_WIDTH - 1 and Bs % SUBLANES == 0

    c = IN_COLS
    proj_w = [ln1[0][None, :],
              w_in[0][:, :MAIN_W].astype(BF16),
              w_in[0][:, c[9]:].astype(BF16),
              jnp.pad(w_in[0][:, c[8]:c[9]], ((0, 0), (0, LANES - GLA_GATE_RANK))).astype(BF16),
              jnp.pad(w_gate_up[0], ((0, LANES - GLA_GATE_RANK), (0, 0))).astype(BF16),
              b_gate_up[0][None, :]]
    post_w = [jnp.concatenate([g_ret[0], g_gla[0]])[None, :], w_ret_out[0].astype(BF16),
              w_gla_out[0].astype(BF16), w_o[0].astype(BF16), ln2[0][None, :],
              w_up[0].astype(BF16), conv_w[0], conv_b[0][None, :], w_down[0].astype(BF16),
              ln_f[None, :]]
    proj_w_specs = [_const_spec(w.shape) for w in proj_w]
    post_w_specs = [_const_spec(w.shape) for w in post_w]
    cos_p, sin_p = _rope_tables(jnp.arange(Lp, dtype=F32))
    cos_s, sin_s = _rope_tables(PAST_LEN + jnp.arange(Ls, dtype=F32))
    sds = lambda *shape: jax.ShapeDtypeStruct(shape, F32)
    full = lambda *shape: pl.BlockSpec(shape, lambda *_: (0,) * len(shape))

    T = 256
    grid_p = (Bp, Lp // T)
    rows_p = lambda w: pl.BlockSpec((None, T, w), lambda b, t: (b, t, 0))
    state_p = lambda *shape: pl.BlockSpec((None,) + shape, lambda b, t: (b,) + (0,) * len(shape))
    rope_spec = pl.BlockSpec((T, RET_DK), lambda b, t: (t, 0))
    scan_p, gate_p, la_p = pl.pallas_call(
        _proj_prompt_kernel, grid=grid_p,
        in_specs=[rows_p(D_MODEL), rope_spec, rope_spec] + proj_w_specs,
        out_specs=[rows_p(SCAN_W), rows_p(GATE_W), rows_p(GLA_QK_W)],
        out_shape=[sds(Bp, Lp, SCAN_W), sds(Bp, Lp, GATE_W), sds(Bp, Lp, GLA_QK_W)],
        compiler_params=_params(2), name="proj_prompt",
    )(x_prompt, cos_p, sin_p, *proj_w)

    o_p, sret_p, sgla_p = pl.pallas_call(
        _scan_prompt_kernel, grid=grid_p,
        in_specs=[rows_p(SCAN_W), rows_p(GLA_QK_W)],
        out_specs=[rows_p(O_W), state_p(RET_HEADS, RET_DK, RET_DV),
                   state_p(GLA_PAIRS, PAIR_DK, GLA_DV)],
        out_shape=[sds(Bp, Lp, O_W), sds(Bp, RET_HEADS, RET_DK, RET_DV),
                   sds(Bp, GLA_PAIRS, PAIR_DK, GLA_DV)],
        scratch_shapes=[
            pltpu.VMEM((T // RET_CHUNK, RET_HEADS, RET_DK + RET_CHUNK, RET_DV), BF16),
            pltpu.VMEM((T // CHUNK, GLA_PAIRS, PAIR_DK + 2 * CHUNK, GLA_DV), BF16)],
        compiler_params=_params(2), name="scan_prompt",
    )(scan_p, la_p)

    y_p, tail_p = pl.pallas_call(
        _post_prompt_kernel, grid=grid_p,
        in_specs=[rows_p(O_W), rows_p(GATE_W), rows_p(D_MODEL)] + post_w_specs,
        out_specs=[rows_p(D_MODEL), state_p(CONV_WIDTH - 1, D_FF)],
        out_shape=[sds(Bp, Lp, D_MODEL), sds(Bp, CONV_WIDTH - 1, D_FF)],
        scratch_shapes=[pltpu.VMEM((T + SUBLANES, D_FF), F32)],
        compiler_params=_params(2), name="post_prompt",
    )(o_p, gate_p, x_prompt, *post_w)

    scan_s, gate_s, la_s = pl.pallas_call(
        _proj_sample_kernel, grid=(1,),
        in_specs=[_const_spec((Bs, Ls, D_MODEL)), full(Ls, RET_DK), full(Ls, RET_DK)]
        + proj_w_specs,
        out_specs=[full(Bs, Ls * SCAN_W), full(Bs, Ls * GATE_W), full(Bs, Ls * GLA_QK_W)],
        out_shape=[sds(Bs, Ls * SCAN_W), sds(Bs, Ls * GATE_W), sds(Bs, Ls * GLA_QK_W)],
        compiler_params=_params(1), name="proj_sample",
    )(x_sample, cos_s, sin_s, *proj_w)

    SB = 16
    rows_s = lambda w: pl.BlockSpec((SB, w), lambda i: (i, 0))
    state_s = lambda *shape: pl.BlockSpec((SB,) + shape, lambda i: (i,) + (0,) * len(shape))
    o_s, sret_s, sgla_s = pl.pallas_call(
        functools.partial(_scan_sample_kernel, Ls), grid=(Bs // SB,),
        in_specs=[rows_s(Ls * SCAN_W), rows_s(Ls * GLA_QK_W),
                  state_s(RET_HEADS, RET_DK, RET_DV), state_s(GLA_PAIRS, PAIR_DK, GLA_DV)],
        out_specs=[rows_s(Ls * O_W), state_s(RET_HEADS, RET_DK, RET_DV),
                   state_s(GLA_PAIRS, PAIR_DK, GLA_DV)],
        out_shape=[sds(Bs, Ls * O_W), sds(Bs, RET_HEADS, RET_DK, RET_DV),
                   sds(Bs, GLA_PAIRS, PAIR_DK, GLA_DV)],
        compiler_params=_params(1), name="scan_sample",
    )(scan_s, la_s, state_ret[0], state_gla[0].reshape(Bs, GLA_PAIRS, PAIR_DK, GLA_DV))

    y_s, tail_s = pl.pallas_call(
        _post_sample_kernel, grid=(1,),
        in_specs=[_const_spec(s) for s in ((Bs, Ls * O_W), (Bs, Ls * GATE_W), (Bs, Ls, D_MODEL),
                                           (Bs, CONV_WIDTH - 1, D_FF))] + post_w_specs,
        out_specs=[full(Bs, Ls, D_MODEL), full(Bs, CONV_WIDTH - 1, D_FF)],
        out_shape=[sds(Bs, Ls, D_MODEL), sds(Bs, CONV_WIDTH - 1, D_FF)],
        scratch_shapes=[pltpu.VMEM((CONV_WIDTH - 1, Bs, D_FF), F32)],
        compiler_params=_params(1), name="post_sample",
    )(o_s, gate_s, x_sample, cache_conv[0], *post_w)

    gshape = (1, -1, GLA_HEADS, GLA_DK, GLA_DV)
    return (y_p, y_s, sret_p[None], sret_s[None], sgla_p.reshape(gshape), sgla_s.reshape(gshape),
            tail_p[None], tail_s[None])
```

```python
import functools
import math

import jax
import jax.numpy as jnp
import numpy as np
from jax import lax
from jax.experimental import pallas as pl
from jax.experimental.pallas import tpu as pltpu

D_MODEL = 1024
PAST_LEN = 16384
RET_HEADS = 4
RET_DK = 128
RET_DV = 128
GLA_HEADS = 4
GLA_DK = 64
GLA_DV = 128
GLA_GATE_RANK = 16
GLA_GATE_TAU = 16.0
D_FF = 2816
CONV_WIDTH = 3
CHUNK = 64
RET_CHUNK = 128
ROPE_BASE = 10000.0
EPS = 1e-6

LANES = 128
SUBLANES = 8
VMEM_LIMIT = 56 * 1024 * 1024

RET_W = RET_HEADS * RET_DK
GLA_QK_W = GLA_HEADS * GLA_DK
GLA_V_W = GLA_HEADS * GLA_DV
GLA_PAIRS = GLA_HEADS // 2
PAIR_DK = 2 * GLA_DK
SCAN_W = 3 * RET_W + 2 * GLA_QK_W + GLA_V_W
GATE_W = RET_W + GLA_V_W + 2 * D_MODEL
O_W = RET_W + GLA_V_W
FFN_BLOCK = 256
MERGE_BLOCK = 256
LOG_GAMMA = tuple(math.log(1.0 - 2.0 ** (-5.0 - h)) for h in range(RET_HEADS))
IN_COLS = tuple(np.cumsum((0, RET_W, RET_W, RET_W, RET_W, GLA_QK_W, GLA_QK_W, GLA_V_W, GLA_V_W,
                           GLA_GATE_RANK, D_MODEL, D_MODEL)).tolist())
MAIN_W = IN_COLS[8]

BF16 = jnp.bfloat16
F32 = jnp.float32


def _dot(a, b):
    return jnp.dot(a.astype(BF16), b.astype(BF16), preferred_element_type=F32)


def _dot_nt(a, b):
    return lax.dot_general(a, b, (((1,), (1,)), ((), ())), preferred_element_type=F32)


def _dot_tn(a, b):
    return lax.dot_general(a, b, (((0,), (0,)), ((), ())), preferred_element_type=F32)


def _split3(x):
    hi = x.astype(BF16)
    r = x - hi.astype(F32)
    mid = r.astype(BF16)
    lo = (r - mid.astype(F32)).astype(BF16)
    return hi, mid, lo


def _dot_exact_lhs01(m01, x):
    m = m01.astype(BF16)
    return sum(jnp.dot(m, t, preferred_element_type=F32) for t in _split3(x))


def _dot_tn_exact_rhs01(x, m01):
    m = m01.astype(BF16)
    return sum(_dot_tn(t, m) for t in _split3(x))


def _rms(x, g):
    return x * lax.rsqrt(jnp.mean(x * x, axis=-1, keepdims=True) + EPS) * g


def _sigmoid(x):
    return 1.0 / (1.0 + jnp.exp2(x * (-1.0 / math.log(2.0))))


def _proj_rows(x, cos, sin, ln1_ref, wmain_ref, wm_ref, wga_ref, wgu_ref, bgu_ref,
               put_scan, put_gate, put_la):
    n = _rms(x, ln1_ref[...]).astype(BF16)
    mm = lambda ref, lo, hi: jnp.dot(n, ref[:, lo:hi], preferred_element_type=F32)
    c = IN_COLS
    qk = mm(wmain_ref, c[0], c[2])
    ga = jnp.dot(n, wga_ref[...], preferred_element_type=F32)
    z = _dot(ga, wgu_ref[...]) + bgu_ref[...]
    log_sig = -(jnp.maximum(-z, 0.0) + jnp.log1p(jnp.exp(-jnp.abs(z))))
    put_la(0, log_sig / GLA_GATE_TAU)
    put_scan(2 * RET_W, mm(wmain_ref, c[2], c[3]))
    gqk = mm(wmain_ref, c[4], c[6])
    put_scan(3 * RET_W, gqk[:, :GLA_QK_W] * (GLA_DK ** -0.5))
    put_scan(3 * RET_W + GLA_QK_W, gqk[:, GLA_QK_W:])
    put_scan(3 * RET_W + 2 * GLA_QK_W, mm(wmain_ref, c[6], c[7]))
    put_gate(0, mm(wmain_ref, c[3], c[4]))
    put_gate(RET_W, mm(wmain_ref, c[7], c[8]))
    put_gate(O_W, jnp.dot(n, wm_ref[...], preferred_element_type=F32))
    for g in range(2 * RET_HEADS):
        lo = g * RET_DK
        t = qk[:, lo:lo + RET_DK]
        r = t * cos + pltpu.roll(t, RET_DK // 2, axis=1) * sin
        if g >= RET_HEADS:
            r = r * (RET_DK ** -0.5)
        put_scan(lo, r)


def _putter(ref, base=0):
    def put(lo, val):
        ref[:, base + lo:base + lo + val.shape[1]] = val
    return put


def _proj_sample_kernel(x_ref, cos_ref, sin_ref, *rest):
    w_refs, (scan_ref, gate_ref, la_ref) = rest[:6], rest[6:]
    for j in range(x_ref.shape[1]):
        _proj_rows(x_ref[:, j, :], cos_ref[j:j + 1, :], sin_ref[j:j + 1, :], *w_refs,
                   _putter(scan_ref, j * SCAN_W), _putter(gate_ref, j * GATE_W),
                   _putter(la_ref, j * GLA_QK_W))


def _attn_prompt_kernel(x_ref, cos_ref, sin_ref, *rest):
    w_refs = rest[:6]
    gate_ref, o_ref, sret_ref, sgla_ref, qkv_ref, la_ref, rhs_ret_ref, rhs_gla_ref = rest[6:]
    _proj_rows(x_ref[...], cos_ref[...], sin_ref[...], *w_refs,
               _putter(qkv_ref), _putter(gate_ref), _putter(la_ref))

    @pl.when(pl.program_id(1) == 0)
    def _():
        sret_ref[...] = jnp.zeros_like(sret_ref)
        sgla_ref[...] = jnp.zeros_like(sgla_ref)

    _scan_prompt_rows(qkv_ref, la_ref, o_ref, sret_ref, sgla_ref, rhs_ret_ref, rhs_gla_ref)


def _scan_prompt_rows(qkv_ref, la_ref, o_ref, sret_ref, sgla_ref, rhs_ret_ref, rhs_gla_ref):
    T = qkv_ref.shape[0]
    n_rc = T // RET_CHUNK
    n_gc = T // CHUNK

    ri = lax.broadcasted_iota(jnp.int32, (RET_CHUNK, RET_CHUNK), 0)
    ci = lax.broadcasted_iota(jnp.int32, (RET_CHUNK, RET_CHUNK), 1)
    diff = (ri - ci).astype(F32)
    pos = (lax.broadcasted_iota(jnp.int32, (T, RET_DK), 0) % RET_CHUNK).astype(F32)
    ret_lhs, ret_upd = [], []
    for h in range(RET_HEADS):
        lg = LOG_GAMMA[h]
        q = qkv_ref[:, h * RET_DK:(h + 1) * RET_DK]
        k = qkv_ref[:, RET_W + h * RET_DK:RET_W + (h + 1) * RET_DK]
        vb = qkv_ref[:, 2 * RET_W + h * RET_DV:2 * RET_W + (h + 1) * RET_DV].astype(BF16)
        qb = q.astype(BF16)
        kb = k.astype(BF16)
        qd = (q * jnp.exp(lg * (pos + 1.0))).astype(BF16)
        kt = (k * jnp.exp(lg * (RET_CHUNK - 1.0 - pos))).astype(BF16)
        dm = jnp.where(ri >= ci, jnp.exp(lg * diff), 0.0)
        for c in range(n_rc):
            rc = slice(c * RET_CHUNK, (c + 1) * RET_CHUNK)
            rhs_ret_ref[c, h, RET_DK:, :] = vb[rc]
            ret_upd.append(_dot_tn(kt[rc], vb[rc]))
            s = _dot_nt(qb[rc], kb[rc]) * dm
            ret_lhs.append(jnp.concatenate([qd[rc], s.astype(BF16)], axis=1))

    ti = lax.broadcasted_iota(jnp.int32, (T, T), 0)
    tj = lax.broadcasted_iota(jnp.int32, (T, T), 1)
    in_chunk_causal = ((ti // CHUNK) == (tj // CHUNK)) & (ti >= tj)
    go = 3 * RET_W
    bc = _dot_exact_lhs01(in_chunk_causal, la_ref[...])
    bt = jnp.concatenate(
        [jnp.broadcast_to(bc[(c + 1) * CHUNK - 1:(c + 1) * CHUNK, :], (CHUNK, GLA_QK_W))
         for c in range(n_gc)], axis=0)
    gk = qkv_ref[:, go + GLA_QK_W:go + 2 * GLA_QK_W]
    qd_all = qkv_ref[:, go:go + GLA_QK_W] * jnp.exp(bc)
    kd_all = (gk * jnp.exp(-bc)).astype(BF16)
    kt_all = (gk * jnp.exp(bt - bc)).astype(BF16)
    sel = (lax.broadcasted_iota(jnp.int32, (T, LANES), 0)
           == lax.broadcasted_iota(jnp.int32, (T, LANES), 1) * CHUNK + (CHUNK - 1))
    bl = _dot_tn_exact_rhs01(bc, sel)
    lane = lax.broadcasted_iota(jnp.int32, (1, PAIR_DK), 1)
    krow = lax.broadcasted_iota(jnp.int32, (PAIR_DK, 1), 0)
    si = lax.broadcasted_iota(jnp.int32, (CHUNK, 2 * CHUNK), 0)
    sj = lax.broadcasted_iota(jnp.int32, (CHUNK, 2 * CHUNK), 1)
    half_causal = [(sj // CHUNK == half) & (si >= sj % CHUNK) for half in range(2)]
    gla_lhs, gla_upd, gla_dec = [], [], []
    for p in range(GLA_PAIRS):
        cs = slice(p * PAIR_DK, (p + 1) * PAIR_DK)
        vo = go + 2 * GLA_QK_W + p * 2 * GLA_DV
        vp = qkv_ref[:, vo:vo + 2 * GLA_DV].astype(BF16)
        for c in range(n_gc):
            rc = slice(c * CHUNK, (c + 1) * CHUNK)
            rhs_gla_ref[c, p, PAIR_DK:PAIR_DK + CHUNK, :] = vp[rc, :GLA_DV]
            rhs_gla_ref[c, p, PAIR_DK + CHUNK:, :] = vp[rc, GLA_DV:]
            upd = _dot_tn(kt_all[rc, cs], vp[rc])
            gla_upd.append(jnp.where(krow < GLA_DK, upd[:, :GLA_DV], upd[:, GLA_DV:]))
            gla_dec.append(jnp.exp(jnp.broadcast_to(bl[cs, c:c + 1], (PAIR_DK, GLA_DV))))
            kk = jnp.concatenate([kd_all[rc, cs]] * 2, axis=0)
            blocks = []
            for half in range(2):
                qm = jnp.where((lane // GLA_DK) == half, qd_all[rc, cs], 0.0).astype(BF16)
                s = jnp.where(half_causal[half], _dot_nt(qm, kk), 0.0)
                blocks.append(jnp.concatenate([qm, s.astype(BF16)], axis=1))
            gla_lhs.append(jnp.concatenate(blocks, axis=0))

    for h in range(RET_HEADS):
        S = sret_ref[h]
        for c in range(n_rc):
            rhs_ret_ref[c, h, :RET_DK, :] = S.astype(BF16)
            S = math.exp(LOG_GAMMA[h] * RET_CHUNK) * S + ret_upd[h * n_rc + c]
        sret_ref[h] = S
    for p in range(GLA_PAIRS):
        S = sgla_ref[p]
        for c in range(n_gc):
            rhs_gla_ref[c, p, :PAIR_DK, :] = S.astype(BF16)
            S = gla_dec[p * n_gc + c] * S + gla_upd[p * n_gc + c]
        sgla_ref[p] = S
    for h in range(RET_HEADS):
        for c in range(n_rc):
            o_ref[c * RET_CHUNK:(c + 1) * RET_CHUNK, h * RET_DV:(h + 1) * RET_DV] = jnp.dot(
                ret_lhs[h * n_rc + c], rhs_ret_ref[c, h], preferred_element_type=F32)
    for p in range(GLA_PAIRS):
        for c in range(n_gc):
            o2 = jnp.dot(gla_lhs[p * n_gc + c], rhs_gla_ref[c, p], preferred_element_type=F32)
            for half in range(2):
                co = RET_W + (2 * p + half) * GLA_DV
                o_ref[c * CHUNK:(c + 1) * CHUNK, co:co + GLA_DV] = o2[half * CHUNK:(half + 1) * CHUNK]


def _scan_sample_kernel(n_pos, qkv_ref, la_ref, sret_in_ref, sgla_in_ref, o_ref, sret_ref, sgla_ref):
    n_groups = qkv_ref.shape[0] // SUBLANES
    R = n_pos * SUBLANES
    seq_of_row = lax.broadcasted_iota(jnp.int32, (R, 1), 0) % SUBLANES
    seq_of_row2 = lax.broadcasted_iota(jnp.int32, (2 * R, 1), 0) % SUBLANES
    lane = lax.broadcasted_iota(jnp.int32, (1, PAIR_DK), 1)
    krow = lax.broadcasted_iota(jnp.int32, (PAIR_DK, 1), 0)
    eye = (lax.broadcasted_iota(jnp.int32, (SUBLANES, LANES), 0)
           == lax.broadcasted_iota(jnp.int32, (SUBLANES, LANES), 1))
    stack = lambda ts: jnp.concatenate(ts, axis=0)

    def body(g, carry):
        rows = pl.ds(pl.multiple_of(g * SUBLANES, SUBLANES), SUBLANES)
        col = lambda j, lo, w: qkv_ref[rows, j * SCAN_W + lo:j * SCAN_W + lo + w]

        for h in range(RET_HEADS):
            lg = LOG_GAMMA[h]
            q = [col(j, h * RET_DK, RET_DK) for j in range(n_pos)]
            k = [col(j, RET_W + h * RET_DK, RET_DK) for j in range(n_pos)]
            v = [col(j, 2 * RET_W + h * RET_DV, RET_DV) for j in range(n_pos)]
            intra = []
            for i in range(n_pos):
                acc = None
                for j in range(i + 1):
                    s = jnp.sum(q[i] * k[j], axis=-1, keepdims=True) * math.exp(lg * (i - j))
                    acc = s * v[j] if acc is None else acc + s * v[j]
                intra.append(acc)
            qd = stack([q[i] * math.exp(lg * (i + 1)) for i in range(n_pos)]).astype(BF16)
            kt = stack([k[j] * math.exp(lg * (n_pos - 1 - j)) for j in range(n_pos)])
            vs = stack(v).astype(BF16)
            inter = jnp.zeros((R, RET_DV), F32)
            for r in range(SUBLANES):
                mine = seq_of_row == r
                S = sret_in_ref[g * SUBLANES + r, h]
                inter = jnp.where(mine, jnp.dot(qd, S.astype(BF16), preferred_element_type=F32),
                                  inter)
                upd = _dot_tn(jnp.where(mine, kt, 0.0).astype(BF16), vs)
                sret_ref[g * SUBLANES + r, h] = math.exp(lg * n_pos) * S + upd
            for i in range(n_pos):
                o_ref[rows, i * O_W + h * RET_DV:i * O_W + (h + 1) * RET_DV] = (
                    intra[i] + inter[i * SUBLANES:(i + 1) * SUBLANES])

        go = 3 * RET_W
        for p in range(GLA_PAIRS):
            la = [la_ref[rows, j * GLA_QK_W + p * PAIR_DK:j * GLA_QK_W + (p + 1) * PAIR_DK]
                  for j in range(n_pos)]
            bc = [la[0]]
            for j in range(1, n_pos):
                bc.append(bc[-1] + la[j])
            bt = bc[-1]
            gq = [col(j, go + p * PAIR_DK, PAIR_DK) for j in range(n_pos)]
            gk = [col(j, go + GLA_QK_W + p * PAIR_DK, PAIR_DK) for j in range(n_pos)]
            vp = [col(j, go + 2 * GLA_QK_W + p * 2 * GLA_DV, 2 * GLA_DV) for j in range(n_pos)]
            qd = [gq[j] * jnp.exp(bc[j]) for j in range(n_pos)]
            kd = [gk[j] * jnp.exp(-bc[j]) for j in range(n_pos)]
            kt = stack([gk[j] * jnp.exp(bt - bc[j]) for j in range(n_pos)])
            halves = [(lane // GLA_DK) == half for half in range(2)]
            intra = [[None] * n_pos for _ in range(2)]
            for i in range(n_pos):
                for j in range(i + 1):
                    prod = qd[i] * kd[j]
                    for half in range(2):
                        s = jnp.sum(jnp.where(halves[half], prod, 0.0), axis=-1, keepdims=True)
                        term = s * vp[j][:, half * GLA_DV:(half + 1) * GLA_DV]
                        intra[half][i] = term if intra[half][i] is None else intra[half][i] + term
            qs = stack(qd)
            lhs = stack([jnp.where(halves[half], qs, 0.0) for half in range(2)]).astype(BF16)
            vs = stack(vp).astype(BF16)
            btt = _dot_tn_exact_rhs01(bt, eye)
            inter = jnp.zeros((2 * R, GLA_DV), F32)
            for r in range(SUBLANES):
                S = sgla_in_ref[g * SUBLANES + r, p]
                inter = jnp.where(seq_of_row2 == r,
                                  jnp.dot(lhs, S.astype(BF16), preferred_element_type=F32), inter)
                upd = _dot_tn(jnp.where(seq_of_row == r, kt, 0.0).astype(BF16), vs)
                upd = jnp.where(krow < GLA_DK, upd[:, :GLA_DV], upd[:, GLA_DV:])
                dec = jnp.exp(jnp.broadcast_to(btt[:, r:r + 1], (PAIR_DK, GLA_DV)))
                sgla_ref[g * SUBLANES + r, p] = dec * S + upd
            for half in range(2):
                for i in range(n_pos):
                    co = i * O_W + RET_W + (2 * p + half) * GLA_DV
                    lo = half * R + i * SUBLANES
                    o_ref[rows, co:co + GLA_DV] = intra[half][i] + inter[lo:lo + SUBLANES]
        return carry

    lax.fori_loop(0, n_groups, body, 0)


def _post_rows(o_ref, gate_ref, x, gn_ref, wro_ref, wgo_ref, wo_ref, ln2_ref, wup_ref, cw_ref,
               cb_ref, wdn_ref, lnf_ref, conv_taps):
    acts = []
    for h in range(RET_HEADS + GLA_HEADS):
        cs = slice(h * LANES, (h + 1) * LANES)
        oh = o_ref[:, cs]
        yh = oh * lax.rsqrt(jnp.mean(oh * oh, axis=-1, keepdims=True) + EPS) * gn_ref[:, cs]
        gh = gate_ref[:, cs]
        acts.append((yh * (gh * _sigmoid(gh))).astype(BF16))
    a_r = jnp.concatenate(acts[:RET_HEADS], axis=1)
    a_g = jnp.concatenate(acts[RET_HEADS:], axis=1)
    blocks = [slice(lo, lo + MERGE_BLOCK) for lo in range(0, D_MODEL, MERGE_BLOCK)]

    def out_block(cs):
        return (jnp.dot(a_r, wro_ref[:, cs], preferred_element_type=F32),
                jnp.dot(a_g, wgo_ref[:, cs], preferred_element_type=F32))

    mix = []
    y_next = out_block(blocks[0])
    for i, cs in enumerate(blocks):
        y_r, y_g = y_next
        if i + 1 < len(blocks):
            y_next = out_block(blocks[i + 1])
        m_r = gate_ref[:, O_W + cs.start:O_W + cs.stop]
        m_g = gate_ref[:, O_W + D_MODEL + cs.start:O_W + D_MODEL + cs.stop]
        mix.append((_sigmoid(m_r) * y_r + _sigmoid(m_g) * y_g).astype(BF16))
    mix = jnp.concatenate(mix, axis=1)
    h1 = jnp.concatenate(
        [x[:, cs] + jnp.dot(mix, wo_ref[:, cs], preferred_element_type=F32) for cs in blocks],
        axis=1)
    n2 = _rms(h1, ln2_ref[...]).astype(BF16)

    nb = D_FF // FFN_BLOCK

    def up_block(j):
        lo = j * FFN_BLOCK
        return (jnp.dot(n2, wup_ref[:, lo:lo + FFN_BLOCK], preferred_element_type=F32),
                jnp.dot(n2, wup_ref[:, D_FF + lo:D_FF + lo + FFN_BLOCK],
                        preferred_element_type=F32))

    h2 = h1
    uv_next = up_block(0)
    for j in range(nb):
        cs = slice(j * FFN_BLOCK, (j + 1) * FFN_BLOCK)
        u, vv = uv_next
        if j + 1 < nb:
            uv_next = up_block(j + 1)
        u_m2, u_m1 = conv_taps(cs, u)
        uc = cb_ref[:, cs] + u_m2 * cw_ref[0:1, cs]
        uc = uc + u_m1 * cw_ref[1:2, cs]
        uc = uc + u * cw_ref[2:3, cs]
        act = 0.5 * uc * (1.0 + lax.erf(uc * float(np.float32(np.sqrt(0.5))))) * vv
        h2 = h2 + _dot(act, wdn_ref[cs, :])
    return _rms(h2, lnf_ref[...])


def _post_prompt_kernel(o_ref, gate_ref, x_ref, *rest):
    w_refs, (y_ref, tail_ref, ubuf_ref) = rest[:10], rest[10:]
    T = x_ref.shape[0]

    @pl.when(pl.program_id(1) == 0)
    def _():
        ubuf_ref[0:SUBLANES, :] = jnp.zeros((SUBLANES, D_FF), F32)

    def conv_taps(cs, u):
        ubuf_ref[SUBLANES:SUBLANES + T, cs] = u
        return (ubuf_ref[SUBLANES - 2:SUBLANES - 2 + T, cs],
                ubuf_ref[SUBLANES - 1:SUBLANES - 1 + T, cs])

    y_ref[...] = _post_rows(o_ref, gate_ref, x_ref[...], *w_refs, conv_taps)
    tail_ref[...] = ubuf_ref[T + SUBLANES - (CONV_WIDTH - 1):T + SUBLANES, :]
    ubuf_ref[0:SUBLANES, :] = ubuf_ref[T:T + SUBLANES, :]


def _post_sample_kernel(o_ref, gate_ref, x_ref, cache_ref, *rest):
    w_refs, (y_ref, tail_ref, ubuf_ref) = rest[:10], rest[10:]
    for k in range(CONV_WIDTH - 1):
        ubuf_ref[k] = cache_ref[:, k, :]

    def conv_taps(cs, u):
        u_m2 = ubuf_ref[0, :, cs]
        u_m1 = ubuf_ref[1, :, cs]
        ubuf_ref[0, :, cs] = u_m1
        ubuf_ref[1, :, cs] = u
        return u_m2, u_m1

    for j in range(x_ref.shape[1]):
        y_ref[:, j, :] = _post_rows(o_ref.at[:, j * O_W:(j + 1) * O_W],
                                    gate_ref.at[:, j * GATE_W:(j + 1) * GATE_W], x_ref[:, j, :],
                                    *w_refs, conv_taps)
    for k in range(CONV_WIDTH - 1):
        tail_ref[:, k, :] = ubuf_ref[k]


def _const_spec(shape):
    nd = len(shape)
    return pl.BlockSpec(shape, lambda *_: (0,) * nd, pipeline_mode=pl.Buffered(1))


def _params(n_grid):
    return pltpu.CompilerParams(dimension_semantics=("arbitrary",) * n_grid,
                                vmem_limit_bytes=VMEM_LIMIT)


def _rope_tables(pos):
    half = RET_DK // 2
    inv = ROPE_BASE ** (-jnp.arange(half, dtype=F32) / half)
    ang = pos[:, None] * inv[None, :]
    cos, sin = jnp.cos(ang), jnp.sin(ang)
    return jnp.concatenate([cos, cos], axis=-1), jnp.concatenate([-sin, sin], axis=-1)


def kernel(x_prompt, x_sample, state_ret, state_gla, cache_conv, ln1, w_in, w_gate_up, b_gate_up,
           g_ret, g_gla, w_ret_out, w_gla_out, w_o, ln2, w_up, conv_w, conv_b, w_down, ln_f):
    Bp, Lp, _ = x_prompt.shape
    Bs, Ls, _ = x_sample.shape
    assert state_ret.shape[0] == 1, "single layer"
    assert Ls >= CONV_WIDTH - 1 and Bs % SUBLANES == 0

    c = IN_COLS
    proj_w = [ln1[0][None, :],
              w_in[0][:, :MAIN_W].astype(BF16),
              w_in[0][:, c[9]:].astype(BF16),
              jnp.pad(w_in[0][:, c[8]:c[9]], ((0, 0), (0, LANES - GLA_GATE_RANK))).astype(BF16),
              jnp.pad(w_gate_up[0], ((0, LANES - GLA_GATE_RANK), (0, 0))).astype(BF16),
              b_gate_up[0][None, :]]
    post_w = [jnp.concatenate([g_ret[0], g_gla[0]])[None, :], w_ret_out[0].astype(BF16),
              w_gla_out[0].astype(BF16), w_o[0].astype(BF16), ln2[0][None, :],
              w_up[0].astype(BF16), conv_w[0], conv_b[0][None, :], w_down[0].astype(BF16),
              ln_f[None, :]]
    proj_w_specs = [_const_spec(w.shape) for w in proj_w]
    post_w_specs = [_const_spec(w.shape) for w in post_w]
    cos_p, sin_p = _rope_tables(jnp.arange(Lp, dtype=F32))
    cos_s, sin_s = _rope_tables(PAST_LEN + jnp.arange(Ls, dtype=F32))
    sds = lambda *shape: jax.ShapeDtypeStruct(shape, F32)
    full = lambda *shape: pl.BlockSpec(shape, lambda *_: (0,) * len(shape))

    T = 256
    grid_p = (Bp, Lp // T)
    rows_p = lambda w: pl.BlockSpec((None, T, w), lambda b, t: (b, t, 0))
    state_p = lambda *shape: pl.BlockSpec((None,) + shape, lambda b, t: (b,) + (0,) * len(shape))
    rope_spec = pl.BlockSpec((T, RET_DK), lambda b, t: (t, 0))
    gate_p, o_p, sret_p, sgla_p = pl.pallas_call(
        _attn_prompt_kernel, grid=grid_p,
        in_specs=[rows_p(D_MODEL), rope_spec, rope_spec] + proj_w_specs,
        out_specs=[rows_p(GATE_W), rows_p(O_W), state_p(RET_HEADS, RET_DK, RET_DV),
                   state_p(GLA_PAIRS, PAIR_DK, GLA_DV)],
        out_shape=[sds(Bp, Lp, GATE_W), sds(Bp, Lp, O_W), sds(Bp, RET_HEADS, RET_DK, RET_DV),
                   sds(Bp, GLA_PAIRS, PAIR_DK, GLA_DV)],
        scratch_shapes=[
            pltpu.VMEM((T, SCAN_W), F32), pltpu.VMEM((T, GLA_QK_W), F32),
            pltpu.VMEM((T // RET_CHUNK, RET_HEADS, RET_DK + RET_CHUNK, RET_DV), BF16),
            pltpu.VMEM((T // CHUNK, GLA_PAIRS, PAIR_DK + 2 * CHUNK, GLA_DV), BF16)],
        compiler_params=_params(2), name="attn_prompt",
    )(x_prompt, cos_p, sin_p, *proj_w)

    y_p, tail_p = pl.pallas_call(
        _post_prompt_kernel, grid=grid_p,
        in_specs=[rows_p(O_W), rows_p(GATE_W), rows_p(D_MODEL)] + post_w_specs,
        out_specs=[rows_p(D_MODEL), state_p(CONV_WIDTH - 1, D_FF)],
        out_shape=[sds(Bp, Lp, D_MODEL), sds(Bp, CONV_WIDTH - 1, D_FF)],
        scratch_shapes=[pltpu.VMEM((T + SUBLANES, D_FF), F32)],
        compiler_params=_params(2), name="post_prompt",
    )(o_p, gate_p, x_prompt, *post_w)

    scan_s, gate_s, la_s = pl.pallas_call(
        _proj_sample_kernel, grid=(1,),
        in_specs=[_const_spec((Bs, Ls, D_MODEL)), full(Ls, RET_DK), full(Ls, RET_DK)]
        + proj_w_specs,
        out_specs=[full(Bs, Ls * SCAN_W), full(Bs, Ls * GATE_W), full(Bs, Ls * GLA_QK_W)],
        out_shape=[sds(Bs, Ls * SCAN_W), sds(Bs, Ls * GATE_W), sds(Bs, Ls * GLA_QK_W)],
        compiler_params=_params(1), name="proj_sample",
    )(x_sample, cos_s, sin_s, *proj_w)

    SB = 16
    rows_s = lambda w: pl.BlockSpec((SB, w), lambda i: (i, 0))
    state_s = lambda *shape: pl.BlockSpec((SB,) + shape, lambda i: (i,) + (0,) * len(shape))
    o_s, sret_s, sgla_s = pl.pallas_call(
        functools.partial(_scan_sample_kernel, Ls), grid=(Bs // SB,),
        in_specs=[rows_s(Ls * SCAN_W), rows_s(Ls * GLA_QK_W),
                  state_s(RET_HEADS, RET_DK, RET_DV), state_s(GLA_PAIRS, PAIR_DK, GLA_DV)],
        out_specs=[rows_s(Ls * O_W), state_s(RET_HEADS, RET_DK, RET_DV),
                   state_s(GLA_PAIRS, PAIR_DK, GLA_DV)],
        out_shape=[sds(Bs, Ls * O_W), sds(Bs, RET_HEADS, RET_DK, RET_DV),
                   sds(Bs, GLA_PAIRS, PAIR_DK, GLA_DV)],
        compiler_params=_params(1), name="scan_sample",
    )(scan_s, la_s, state_ret[0], state_gla[0].reshape(Bs, GLA_PAIRS, PAIR_DK, GLA_DV))

    y_s, tail_s = pl.pallas_call(
        _post_sample_kernel, grid=(1,),
        in_specs=[_const_spec(s) for s in ((Bs, Ls * O_W), (Bs, Ls * GATE_W), (Bs, Ls, D_MODEL),
                                           (Bs, CONV_WIDTH - 1, D_FF))] + post_w_specs,
        out_specs=[full(Bs, Ls, D_MODEL), full(Bs, CONV_WIDTH - 1, D_FF)],
        out_shape=[sds(Bs, Ls, D_MODEL), sds(Bs, CONV_WIDTH - 1, D_FF)],
        scratch_shapes=[pltpu.VMEM((CONV_WIDTH - 1, Bs, D_FF), F32)],
        compiler_params=_params(1), name="post_sample",
    )(o_s, gate_s, x_sample, cache_conv[0], *post_w)

    gshape = (1, -1, GLA_HEADS, GLA_DK, GLA_DV)
    return (y_p, y_s, sret_p[None], sret_s[None], sgla_p.reshape(gshape), sgla_s.reshape(gshape),
            tail_p[None], tail_s[None])
```

```python
import functools
import math

import jax
import jax.numpy as jnp
import numpy as np
from jax import lax
from jax.experimental import pallas as pl
from jax.experimental.pallas import tpu as pltpu

D_MODEL = 1024
PAST_LEN = 16384
RET_HEADS = 4
RET_DK = 128
RET_DV = 128
GLA_HEADS = 4
GLA_DK = 64
GLA_DV = 128
GLA_GATE_RANK = 16
GLA_GATE_TAU = 16.0
D_FF = 2816
CONV_WIDTH = 3
CHUNK = 64
RET_CHUNK = 128
ROPE_BASE = 10000.0
EPS = 1e-6

LANES = 128
SUBLANES = 8
VMEM_LIMIT = 56 * 1024 * 1024

RET_W = RET_HEADS * RET_DK
GLA_QK_W = GLA_HEADS * GLA_DK
GLA_V_W = GLA_HEADS * GLA_DV
GLA_PAIRS = GLA_HEADS // 2
PAIR_DK = 2 * GLA_DK
SCAN_W = 3 * RET_W + 2 * GLA_QK_W + GLA_V_W
GATE_W = RET_W + GLA_V_W + 2 * D_MODEL
O_W = RET_W + GLA_V_W
FFN_BLOCK = 256
MERGE_BLOCK = 256
GATE_BLOCK = 256
LOG_GAMMA = tuple(math.log(1.0 - 2.0 ** (-5.0 - h)) for h in range(RET_HEADS))
IN_COLS = tuple(np.cumsum((0, RET_W, RET_W, RET_W, RET_W, GLA_QK_W, GLA_QK_W, GLA_V_W, GLA_V_W,
                           GLA_GATE_RANK, D_MODEL, D_MODEL)).tolist())
MAIN_W = IN_COLS[8]

BF16 = jnp.bfloat16
F32 = jnp.float32


def _dot(a, b):
    return jnp.dot(a.astype(BF16), b.astype(BF16), preferred_element_type=F32)


def _dot_nt(a, b):
    return lax.dot_general(a, b, (((1,), (1,)), ((), ())), preferred_element_type=F32)


def _dot_tn(a, b):
    return lax.dot_general(a, b, (((0,), (0,)), ((), ())), preferred_element_type=F32)


def _split3(x):
    hi = x.astype(BF16)
    r = x - hi.astype(F32)
    mid = r.astype(BF16)
    lo = (r - mid.astype(F32)).astype(BF16)
    return hi, mid, lo


def _dot_exact_lhs01(m01, x):
    m = m01.astype(BF16)
    return sum(jnp.dot(m, t, preferred_element_type=F32) for t in _split3(x))


def _dot_tn_exact_rhs01(x, m01):
    m = m01.astype(BF16)
    return sum(_dot_tn(t, m) for t in _split3(x))


def _rms(x, g):
    return x * lax.rsqrt(jnp.mean(x * x, axis=-1, keepdims=True) + EPS) * g


def _rms_split(x, g):
    r = lax.rsqrt(jnp.mean(x * x, axis=-1, keepdims=True) + EPS)
    rb = jnp.broadcast_to(r, (x.shape[0], LANES))
    scale = lambda v: v * jnp.concatenate([rb] * (v.shape[1] // LANES), axis=1)
    return (x * g).astype(BF16), scale


def _sigmoid(x):
    return 0.5 * jnp.tanh(0.5 * x) + 0.5


def _proj_rows(x, cos, sin, ln1_ref, wmain_ref, wm_ref, wga_ref, wgu_ref, bgu_ref,
               put_scan, put_gate, put_la):
    n, row_scale = _rms_split(x, ln1_ref[...])
    mm = lambda ref, lo, hi: row_scale(jnp.dot(n, ref[:, lo:hi], preferred_element_type=F32))
    c = IN_COLS
    qk = mm(wmain_ref, c[0], c[2])
    ga = mm(wga_ref, 0, LANES)
    z = _dot(ga, wgu_ref[...]) + bgu_ref[...]
    log_sig = -(jnp.maximum(-z, 0.0) + jnp.log1p(jnp.exp(-jnp.abs(z))))
    put_la(0, log_sig / GLA_GATE_TAU)
    put_scan(2 * RET_W, mm(wmain_ref, c[2], c[3]))
    gqk = mm(wmain_ref, c[4], c[6])
    put_scan(3 * RET_W, gqk[:, :GLA_QK_W] * (GLA_DK ** -0.5))
    put_scan(3 * RET_W + GLA_QK_W, gqk[:, GLA_QK_W:])
    put_scan(3 * RET_W + 2 * GLA_QK_W, mm(wmain_ref, c[6], c[7]))
    for g in range(2 * RET_HEADS):
        lo = g * RET_DK
        t = qk[:, lo:lo + RET_DK]
        r = t * cos + pltpu.roll(t, RET_DK // 2, axis=1) * sin
        if g >= RET_HEADS:
            r = r * (RET_DK ** -0.5)
        put_scan(lo, r)
    jobs = []
    for ref, src, dst, width in ((wmain_ref, c[3], 0, RET_W), (wmain_ref, c[7], RET_W, GLA_V_W),
                                 (wm_ref, 0, O_W, 2 * D_MODEL)):
        for b in range(0, width, GATE_BLOCK):
            jobs.append(functools.partial(
                lambda ref, lo, dst: put_gate(dst, mm(ref, lo, lo + GATE_BLOCK)),
                ref, src + b, dst + b))
    return jobs


def _putter(ref, base=0):
    def put(lo, val):
        ref[:, base + lo:base + lo + val.shape[1]] = val
    return put


def _proj_sample_kernel(x_ref, cos_ref, sin_ref, *rest):
    w_refs, (scan_ref, gate_ref, la_ref) = rest[:6], rest[6:]
    for j in range(x_ref.shape[1]):
        for job in _proj_rows(x_ref[:, j, :], cos_ref[j:j + 1, :], sin_ref[j:j + 1, :], *w_refs,
                              _putter(scan_ref, j * SCAN_W), _putter(gate_ref, j * GATE_W),
                              _putter(la_ref, j * GLA_QK_W)):
            job()


def _attn_prompt_kernel(x_ref, cos_ref, sin_ref, *rest):
    w_refs, gn_ref = rest[:6], rest[6]
    (act_ref, gm_ref, sret_ref, sgla_ref,
     qkv_ref, la_ref, o_ref, og_ref, rhs_ret_ref, rhs_gla_ref) = rest[7:]

    @pl.when(pl.program_id(1) == 0)
    def _():
        sret_ref[...] = jnp.zeros_like(sret_ref)
        sgla_ref[...] = jnp.zeros_like(sgla_ref)

    def put_gate(lo, val):
        if lo < O_W:
            og_ref[:, lo:lo + val.shape[1]] = val
        else:
            gm_ref[:, lo - O_W:lo - O_W + val.shape[1]] = val

    jobs = _proj_rows(x_ref[...], cos_ref[...], sin_ref[...], *w_refs,
                      _putter(qkv_ref), put_gate, _putter(la_ref))

    T = x_ref.shape[0]
    n_heads = RET_HEADS + GLA_HEADS
    n_ticks = RET_HEADS * (T // RET_CHUNK) + GLA_PAIRS * (T // CHUNK) + 2
    n_scan_jobs = len(jobs) - (n_heads - 2)
    ticks = [0]

    def tick():
        ticks[0] += 1
        due = min(ticks[0] * n_scan_jobs // n_ticks, n_scan_jobs)
        while n_scan_jobs - (len(jobs) - (n_heads - 2)) < due:
            jobs.pop(0)()

    _scan_prompt_rows(qkv_ref, la_ref, o_ref, sret_ref, sgla_ref, rhs_ret_ref, rhs_gla_ref, tick)
    for h in range(n_heads):
        if jobs:
            jobs.pop(0)()
        act_ref[:, h * LANES:(h + 1) * LANES] = _head_act(o_ref, og_ref, gn_ref, h)
    assert not jobs


def _scan_prompt_rows(qkv_ref, la_ref, o_ref, sret_ref, sgla_ref, rhs_ret_ref, rhs_gla_ref, tick):
    T = qkv_ref.shape[0]
    n_rc = T // RET_CHUNK
    n_gc = T // CHUNK

    ri = lax.broadcasted_iota(jnp.int32, (RET_CHUNK, RET_CHUNK), 0)
    ci = lax.broadcasted_iota(jnp.int32, (RET_CHUNK, RET_CHUNK), 1)
    diff = (ri - ci).astype(F32)
    pos = (lax.broadcasted_iota(jnp.int32, (T, RET_DK), 0) % RET_CHUNK).astype(F32)
    ret_lhs, ret_upd = [], []
    for h in range(RET_HEADS):
        lg = LOG_GAMMA[h]
        q = qkv_ref[:, h * RET_DK:(h + 1) * RET_DK]
        k = qkv_ref[:, RET_W + h * RET_DK:RET_W + (h + 1) * RET_DK]
        vb = qkv_ref[:, 2 * RET_W + h * RET_DV:2 * RET_W + (h + 1) * RET_DV].astype(BF16)
        qb = q.astype(BF16)
        kb = k.astype(BF16)
        qd = (q * jnp.exp(lg * (pos + 1.0))).astype(BF16)
        kt = (k * jnp.exp(lg * (RET_CHUNK - 1.0 - pos))).astype(BF16)
        dm = jnp.where(ri >= ci, jnp.exp(lg * diff), 0.0)
        for c in range(n_rc):
            rc = slice(c * RET_CHUNK, (c + 1) * RET_CHUNK)
            rhs_ret_ref[c, h, RET_DK:, :] = vb[rc]
            ret_upd.append(_dot_tn(kt[rc], vb[rc]))
            s = _dot_nt(qb[rc], kb[rc]) * dm
            ret_lhs.append(jnp.concatenate([qd[rc], s.astype(BF16)], axis=1))
            tick()

    ti = lax.broadcasted_iota(jnp.int32, (T, T), 0)
    tj = lax.broadcasted_iota(jnp.int32, (T, T), 1)
    in_chunk_causal = ((ti // CHUNK) == (tj // CHUNK)) & (ti >= tj)
    go = 3 * RET_W
    bc = _dot_exact_lhs01(in_chunk_causal, la_ref[...])
    tick()
    bt = jnp.concatenate(
        [jnp.broadcast_to(bc[(c + 1) * CHUNK - 1:(c + 1) * CHUNK, :], (CHUNK, GLA_QK_W))
         for c in range(n_gc)], axis=0)
    gk = qkv_ref[:, go + GLA_QK_W:go + 2 * GLA_QK_W]
    qd_all = qkv_ref[:, go:go + GLA_QK_W] * jnp.exp(bc)
    kd_all = (gk * jnp.exp(-bc)).astype(BF16)
    kt_all = (gk * jnp.exp(bt - bc)).astype(BF16)
    sel = (lax.broadcasted_iota(jnp.int32, (T, LANES), 0)
           == lax.broadcasted_iota(jnp.int32, (T, LANES), 1) * CHUNK + (CHUNK - 1))
    bl = _dot_tn_exact_rhs01(bc, sel)
    tick()
    lane = lax.broadcasted_iota(jnp.int32, (1, PAIR_DK), 1)
    krow = lax.broadcasted_iota(jnp.int32, (PAIR_DK, 1), 0)
    si = lax.broadcasted_iota(jnp.int32, (CHUNK, 2 * CHUNK), 0)
    sj = lax.broadcasted_iota(jnp.int32, (CHUNK, 2 * CHUNK), 1)
    half_causal = [(sj // CHUNK == half) & (si >= sj % CHUNK) for half in range(2)]
    gla_lhs, gla_upd, gla_dec = [], [], []
    for p in range(GLA_PAIRS):
        cs = slice(p * PAIR_DK, (p + 1) * PAIR_DK)
        vo = go + 2 * GLA_QK_W + p * 2 * GLA_DV
        vp = qkv_ref[:, vo:vo + 2 * GLA_DV].astype(BF16)
        for c in range(n_gc):
            rc = slice(c * CHUNK, (c + 1) * CHUNK)
            rhs_gla_ref[c, p, PAIR_DK:PAIR_DK + CHUNK, :] = vp[rc, :GLA_DV]
            rhs_gla_ref[c, p, PAIR_DK + CHUNK:, :] = vp[rc, GLA_DV:]
            upd = _dot_tn(kt_all[rc, cs], vp[rc])
            gla_upd.append(jnp.where(krow < GLA_DK, upd[:, :GLA_DV], upd[:, GLA_DV:]))
            gla_dec.append(jnp.exp(jnp.broadcast_to(bl[cs, c:c + 1], (PAIR_DK, GLA_DV))))
            kk = jnp.concatenate([kd_all[rc, cs]] * 2, axis=0)
            blocks = []
            for half in range(2):
                qm = jnp.where((lane // GLA_DK) == half, qd_all[rc, cs], 0.0).astype(BF16)
                s = jnp.where(half_causal[half], _dot_nt(qm, kk), 0.0)
                blocks.append(jnp.concatenate([qm, s.astype(BF16)], axis=1))
            gla_lhs.append(jnp.concatenate(blocks, axis=0))
            tick()

    for h in range(RET_HEADS):
        S = sret_ref[h]
        for c in range(n_rc):
            rhs_ret_ref[c, h, :RET_DK, :] = S.astype(BF16)
            S = math.exp(LOG_GAMMA[h] * RET_CHUNK) * S + ret_upd[h * n_rc + c]
        sret_ref[h] = S
    for p in range(GLA_PAIRS):
        S = sgla_ref[p]
        for c in range(n_gc):
            rhs_gla_ref[c, p, :PAIR_DK, :] = S.astype(BF16)
            S = gla_dec[p * n_gc + c] * S + gla_upd[p * n_gc + c]
        sgla_ref[p] = S
    for h in range(RET_HEADS):
        for c in range(n_rc):
            o_ref[c * RET_CHUNK:(c + 1) * RET_CHUNK, h * RET_DV:(h + 1) * RET_DV] = jnp.dot(
                ret_lhs[h * n_rc + c], rhs_ret_ref[c, h], preferred_element_type=F32)
    for p in range(GLA_PAIRS):
        for c in range(n_gc):
            o2 = jnp.dot(gla_lhs[p * n_gc + c], rhs_gla_ref[c, p], preferred_element_type=F32)
            for half in range(2):
                co = RET_W + (2 * p + half) * GLA_DV
                o_ref[c * CHUNK:(c + 1) * CHUNK, co:co + GLA_DV] = o2[half * CHUNK:(half + 1) * CHUNK]


def _scan_sample_kernel(n_pos, qkv_ref, la_ref, sret_in_ref, sgla_in_ref, o_ref, sret_ref, sgla_ref):
    n_groups = qkv_ref.shape[0] // SUBLANES
    R = n_pos * SUBLANES
    seq_of_row = lax.broadcasted_iota(jnp.int32, (R, 1), 0) % SUBLANES
    seq_of_row2 = lax.broadcasted_iota(jnp.int32, (2 * R, 1), 0) % SUBLANES
    lane = lax.broadcasted_iota(jnp.int32, (1, PAIR_DK), 1)
    krow = lax.broadcasted_iota(jnp.int32, (PAIR_DK, 1), 0)
    eye = (lax.broadcasted_iota(jnp.int32, (SUBLANES, LANES), 0)
           == lax.broadcasted_iota(jnp.int32, (SUBLANES, LANES), 1))
    stack = lambda ts: jnp.concatenate(ts, axis=0)

    def body(g, carry):
        rows = pl.ds(pl.multiple_of(g * SUBLANES, SUBLANES), SUBLANES)
        col = lambda j, lo, w: qkv_ref[rows, j * SCAN_W + lo:j * SCAN_W + lo + w]

        for h in range(RET_HEADS):
            lg = LOG_GAMMA[h]
            q = [col(j, h * RET_DK, RET_DK) for j in range(n_pos)]
            k = [col(j, RET_W + h * RET_DK, RET_DK) for j in range(n_pos)]
            v = [col(j, 2 * RET_W + h * RET_DV, RET_DV) for j in range(n_pos)]
            intra = []
            for i in range(n_pos):
                acc = None
                for j in range(i + 1):
                    s = jnp.sum(q[i] * k[j], axis=-1, keepdims=True) * math.exp(lg * (i - j))
                    acc = s * v[j] if acc is None else acc + s * v[j]
                intra.append(acc)
            qd = stack([q[i] * math.exp(lg * (i + 1)) for i in range(n_pos)]).astype(BF16)
            kt = stack([k[j] * math.exp(lg * (n_pos - 1 - j)) for j in range(n_pos)])
            vs = stack(v).astype(BF16)
            inter = jnp.zeros((R, RET_DV), F32)
            for r in range(SUBLANES):
                mine = seq_of_row == r
                S = sret_in_ref[g * SUBLANES + r, h]
                inter = jnp.where(mine, jnp.dot(qd, S.astype(BF16), preferred_element_type=F32),
                                  inter)
                upd = _dot_tn(jnp.where(mine, kt, 0.0).astype(BF16), vs)
                sret_ref[g * SUBLANES + r, h] = math.exp(lg * n_pos) * S + upd
            for i in range(n_pos):
                o_ref[rows, i * O_W + h * RET_DV:i * O_W + (h + 1) * RET_DV] = (
                    intra[i] + inter[i * SUBLANES:(i + 1) * SUBLANES])

        go = 3 * RET_W
        for p in range(GLA_PAIRS):
            la = [la_ref[rows, j * GLA_QK_W + p * PAIR_DK:j * GLA_QK_W + (p + 1) * PAIR_DK]
                  for j in range(n_pos)]
            bc = [la[0]]
            for j in range(1, n_pos):
                bc.append(bc[-1] + la[j])
            bt = bc[-1]
            gq = [col(j, go + p * PAIR_DK, PAIR_DK) for j in range(n_pos)]
            gk = [col(j, go + GLA_QK_W + p * PAIR_DK, PAIR_DK) for j in range(n_pos)]
            vp = [col(j, go + 2 * GLA_QK_W + p * 2 * GLA_DV, 2 * GLA_DV) for j in range(n_pos)]
            qd = [gq[j] * jnp.exp(bc[j]) for j in range(n_pos)]
            kd = [gk[j] * jnp.exp(-bc[j]) for j in range(n_pos)]
            kt = stack([gk[j] * jnp.exp(bt - bc[j]) for j in range(n_pos)])
            halves = [(lane // GLA_DK) == half for half in range(2)]
            intra = [[None] * n_pos for _ in range(2)]
            for i in range(n_pos):
                for j in range(i + 1):
                    prod = qd[i] * kd[j]
                    for half in range(2):
                        s = jnp.sum(jnp.where(halves[half], prod, 0.0), axis=-1, keepdims=True)
                        term = s * vp[j][:, half * GLA_DV:(half + 1) * GLA_DV]
                        intra[half][i] = term if intra[half][i] is None else intra[half][i] + term
            qs = stack(qd)
            lhs = stack([jnp.where(halves[half], qs, 0.0) for half in range(2)]).astype(BF16)
            vs = stack(vp).astype(BF16)
            btt = _dot_tn_exact_rhs01(bt, eye)
            inter = jnp.zeros((2 * R, GLA_DV), F32)
            for r in range(SUBLANES):
                S = sgla_in_ref[g * SUBLANES + r, p]
                inter = jnp.where(seq_of_row2 == r,
                                  jnp.dot(lhs, S.astype(BF16), preferred_element_type=F32), inter)
                upd = _dot_tn(jnp.where(seq_of_row == r, kt, 0.0).astype(BF16), vs)
                upd = jnp.where(krow < GLA_DK, upd[:, :GLA_DV], upd[:, GLA_DV:])
                dec = jnp.exp(jnp.broadcast_to(btt[:, r:r + 1], (PAIR_DK, GLA_DV)))
                sgla_ref[g * SUBLANES + r, p] = dec * S + upd
            for half in range(2):
                for i in range(n_pos):
                    co = i * O_W + RET_W + (2 * p + half) * GLA_DV
                    lo = half * R + i * SUBLANES
                    o_ref[rows, co:co + GLA_DV] = intra[half][i] + inter[lo:lo + SUBLANES]
        return carry

    lax.fori_loop(0, n_groups, body, 0)


def _head_act(o_ref, gate_ref, gn_ref, h):
    cs = slice(h * LANES, (h + 1) * LANES)
    oh = o_ref[:, cs]
    yh = oh * lax.rsqrt(jnp.mean(oh * oh, axis=-1, keepdims=True) + EPS) * gn_ref[:, cs]
    gh = gate_ref[:, cs]
    return (yh * (gh * _sigmoid(gh))).astype(BF16)


def _post_rows(a_r, a_g, gm_ref, x, wro_ref, wgo_ref, wo_ref, ln2_ref, wup_ref, cw_ref,
               cb_ref, wdn_ref, lnf_ref, conv_taps):
    blocks = [slice(lo, lo + MERGE_BLOCK) for lo in range(0, D_MODEL, MERGE_BLOCK)]

    def out_block(cs):
        return (jnp.dot(a_r, wro_ref[:, cs], preferred_element_type=F32),
                jnp.dot(a_g, wgo_ref[:, cs], preferred_element_type=F32))

    mix = []
    y_next = out_block(blocks[0])
    for i, cs in enumerate(blocks):
        y_r, y_g = y_next
        if i + 1 < len(blocks):
            y_next = out_block(blocks[i + 1])
        m_r = gm_ref[:, cs]
        m_g = gm_ref[:, D_MODEL + cs.start:D_MODEL + cs.stop]
        mix.append((_sigmoid(m_r) * y_r + _sigmoid(m_g) * y_g).astype(BF16))
    mix = jnp.concatenate(mix, axis=1)
    h1 = jnp.concatenate(
        [x[:, cs] + jnp.dot(mix, wo_ref[:, cs], preferred_element_type=F32) for cs in blocks],
        axis=1)
    n2, row_scale = _rms_split(h1, ln2_ref[...])

    nb = D_FF // FFN_BLOCK

    def up_block(j):
        lo = j * FFN_BLOCK
        return (jnp.dot(n2, wup_ref[:, lo:lo + FFN_BLOCK], preferred_element_type=F32),
                jnp.dot(n2, wup_ref[:, D_FF + lo:D_FF + lo + FFN_BLOCK],
                        preferred_element_type=F32))

    h2 = h1
    uv_next = up_block(0)
    for j in range(nb):
        cs = slice(j * FFN_BLOCK, (j + 1) * FFN_BLOCK)
        u, vv = (row_scale(t) for t in uv_next)
        if j + 1 < nb:
            uv_next = up_block(j + 1)
        u_m2, u_m1 = conv_taps(cs, u)
        uc = cb_ref[:, cs] + u_m2 * cw_ref[0:1, cs]
        uc = uc + u_m1 * cw_ref[1:2, cs]
        uc = uc + u * cw_ref[2:3, cs]
        act = 0.5 * uc * (1.0 + lax.erf(uc * float(np.float32(np.sqrt(0.5))))) * vv
        h2 = h2 + _dot(act, wdn_ref[cs, :])
    return _rms(h2, lnf_ref[...])


def _post_prompt_kernel(act_ref, gm_ref, x_ref, *rest):
    w_refs, (y_ref, tail_ref, ubuf_ref) = rest[:9], rest[9:]
    T = x_ref.shape[0]

    @pl.when(pl.program_id(1) == 0)
    def _():
        ubuf_ref[0:SUBLANES, :] = jnp.zeros((SUBLANES, D_FF), F32)

    def conv_taps(cs, u):
        ubuf_ref[SUBLANES:SUBLANES + T, cs] = u
        return (ubuf_ref[SUBLANES - 2:SUBLANES - 2 + T, cs],
                ubuf_ref[SUBLANES - 1:SUBLANES - 1 + T, cs])

    y_ref[...] = _post_rows(act_ref[:, :RET_W], act_ref[:, RET_W:], gm_ref, x_ref[...], *w_refs,
                            conv_taps)
    tail_ref[...] = ubuf_ref[T + SUBLANES - (CONV_WIDTH - 1):T + SUBLANES, :]
    ubuf_ref[0:SUBLANES, :] = ubuf_ref[T:T + SUBLANES, :]


def _post_sample_kernel(o_ref, gate_ref, x_ref, cache_ref, *rest):
    w_refs, (y_ref, tail_ref, ubuf_ref) = rest[:10], rest[10:]
    for k in range(CONV_WIDTH - 1):
        ubuf_ref[k] = cache_ref[:, k, :]

    def conv_taps(cs, u):
        u_m2 = ubuf_ref[0, :, cs]
        u_m1 = ubuf_ref[1, :, cs]
        ubuf_ref[0, :, cs] = u_m1
        ubuf_ref[1, :, cs] = u
        return u_m2, u_m1

    gn_ref, w_refs = w_refs[0], w_refs[1:]
    for j in range(x_ref.shape[1]):
        o_j = o_ref.at[:, j * O_W:(j + 1) * O_W]
        gate_j = gate_ref.at[:, j * GATE_W:(j + 1) * GATE_W]
        acts = [_head_act(o_j, gate_j, gn_ref, h) for h in range(RET_HEADS + GLA_HEADS)]
        y_ref[:, j, :] = _post_rows(jnp.concatenate(acts[:RET_HEADS], axis=1),
                                    jnp.concatenate(acts[RET_HEADS:], axis=1),
                                    gate_j.at[:, O_W:], x_ref[:, j, :], *w_refs, conv_taps)
    for k in range(CONV_WIDTH - 1):
        tail_ref[:, k, :] = ubuf_ref[k]


def _const_spec(shape):
    nd = len(shape)
    return pl.BlockSpec(shape, lambda *_: (0,) * nd, pipeline_mode=pl.Buffered(1))


def _params(n_grid):
    return pltpu.CompilerParams(dimension_semantics=("arbitrary",) * n_grid,
                                vmem_limit_bytes=VMEM_LIMIT)


def _rope_tables(pos):
    half = RET_DK // 2
    inv = ROPE_BASE ** (-jnp.arange(half, dtype=F32) / half)
    ang = pos[:, None] * inv[None, :]
    cos, sin = jnp.cos(ang), jnp.sin(ang)
    return jnp.concatenate([cos, cos], axis=-1), jnp.concatenate([-sin, sin], axis=-1)


def kernel(x_prompt, x_sample, state_ret, state_gla, cache_conv, ln1, w_in, w_gate_up, b_gate_up,
           g_ret, g_gla, w_ret_out, w_gla_out, w_o, ln2, w_up, conv_w, conv_b, w_down, ln_f):
    Bp, Lp, _ = x_prompt.shape
    Bs, Ls, _ = x_sample.shape
    assert state_ret.shape[0] == 1, "single layer"
    assert Ls >= CONV_WIDTH - 1 and Bs % SUBLANES == 0

    c = IN_COLS
    proj_w = [ln1[0][None, :],
              w_in[0][:, :MAIN_W].astype(BF16),
              w_in[0][:, c[9]:].astype(BF16),
              jnp.pad(w_in[0][:, c[8]:c[9]], ((0, 0), (0, LANES - GLA_GATE_RANK))).astype(BF16),
              jnp.pad(w_gate_up[0], ((0, LANES - GLA_GATE_RANK), (0, 0))).astype(BF16),
              b_gate_up[0][None, :]]
    post_w = [jnp.concatenate([g_ret[0], g_gla[0]])[None, :], w_ret_out[0].astype(BF16),
              w_gla_out[0].astype(BF16), w_o[0].astype(BF16), ln2[0][None, :],
              w_up[0].astype(BF16), conv_w[0], conv_b[0][None, :], w_down[0].astype(BF16),
              ln_f[None, :]]
    proj_w_specs = [_const_spec(w.shape) for w in proj_w]
    post_w_specs = [_const_spec(w.shape) for w in post_w]
    cos_p, sin_p = _rope_tables(jnp.arange(Lp, dtype=F32))
    cos_s, sin_s = _rope_tables(PAST_LEN + jnp.arange(Ls, dtype=F32))
    sds = lambda *shape: jax.ShapeDtypeStruct(shape, F32)
    full = lambda *shape: pl.BlockSpec(shape, lambda *_: (0,) * len(shape))

    T = 256
    grid_p = (Bp, Lp // T)
    rows_p = lambda w: pl.BlockSpec((None, T, w), lambda b, t: (b, t, 0))
    state_p = lambda *shape: pl.BlockSpec((None,) + shape, lambda b, t: (b,) + (0,) * len(shape))
    rope_spec = pl.BlockSpec((T, RET_DK), lambda b, t: (t, 0))
    act_p, gm_p, sret_p, sgla_p = pl.pallas_call(
        _attn_prompt_kernel, grid=grid_p,
        in_specs=[rows_p(D_MODEL), rope_spec, rope_spec] + proj_w_specs + post_w_specs[:1],
        out_specs=[rows_p(O_W), rows_p(2 * D_MODEL), state_p(RET_HEADS, RET_DK, RET_DV),
                   state_p(GLA_PAIRS, PAIR_DK, GLA_DV)],
        out_shape=[jax.ShapeDtypeStruct((Bp, Lp, O_W), BF16), sds(Bp, Lp, 2 * D_MODEL),
                   sds(Bp, RET_HEADS, RET_DK, RET_DV), sds(Bp, GLA_PAIRS, PAIR_DK, GLA_DV)],
        scratch_shapes=[
            pltpu.VMEM((T, SCAN_W), F32), pltpu.VMEM((T, GLA_QK_W), F32),
            pltpu.VMEM((T, O_W), F32), pltpu.VMEM((T, O_W), F32),
            pltpu.VMEM((T // RET_CHUNK, RET_HEADS, RET_DK + RET_CHUNK, RET_DV), BF16),
            pltpu.VMEM((T // CHUNK, GLA_PAIRS, PAIR_DK + 2 * CHUNK, GLA_DV), BF16)],
        compiler_params=_params(2), name="attn_prompt",
    )(x_prompt, cos_p, sin_p, *proj_w, post_w[0])

    y_p, tail_p = pl.pallas_call(
        _post_prompt_kernel, grid=grid_p,
        in_specs=[rows_p(O_W), rows_p(2 * D_MODEL), rows_p(D_MODEL)] + post_w_specs[1:],
        out_specs=[rows_p(D_MODEL), state_p(CONV_WIDTH - 1, D_FF)],
        out_shape=[sds(Bp, Lp, D_MODEL), sds(Bp, CONV_WIDTH - 1, D_FF)],
        scratch_shapes=[pltpu.VMEM((T + SUBLANES, D_FF), F32)],
        compiler_params=_params(2), name="post_prompt",
    )(act_p, gm_p, x_prompt, *post_w[1:])

    scan_s, gate_s, la_s = pl.pallas_call(
        _proj_sample_kernel, grid=(1,),
        in_specs=[_const_spec((Bs, Ls, D_MODEL)), full(Ls, RET_DK), full(Ls, RET_DK)]
        + proj_w_specs,
        out_specs=[full(Bs, Ls * SCAN_W), full(Bs, Ls * GATE_W), full(Bs, Ls * GLA_QK_W)],
        out_shape=[sds(Bs, Ls * SCAN_W), sds(Bs, Ls * GATE_W), sds(Bs, Ls * GLA_QK_W)],
        compiler_params=_params(1), name="proj_sample",
    )(x_sample, cos_s, sin_s, *proj_w)

    SB = 16
    rows_s = lambda w: pl.BlockSpec((SB, w), lambda i: (i, 0))
    state_s = lambda *shape: pl.BlockSpec((SB,) + shape, lambda i: (i,) + (0,) * len(shape))
    o_s, sret_s, sgla_s = pl.pallas_call(
        functools.partial(_scan_sample_kernel, Ls), grid=(Bs // SB,),
        in_specs=[rows_s(Ls * SCAN_W), rows_s(Ls * GLA_QK_W),
                  state_s(RET_HEADS, RET_DK, RET_DV), state_s(GLA_PAIRS, PAIR_DK, GLA_DV)],
        out_specs=[rows_s(Ls * O_W), state_s(RET_HEADS, RET_DK, RET_DV),
                   state_s(GLA_PAIRS, PAIR_DK, GLA_DV)],
        out_shape=[sds(Bs, Ls * O_W), sds(Bs, RET_HEADS, RET_DK, RET_DV),
                   sds(Bs, GLA_PAIRS, PAIR_DK, GLA_DV)],
        compiler_params=_params(1), name="scan_sample",
    )(scan_s, la_s, state_ret[0], state_gla[0].reshape(Bs, GLA_PAIRS, PAIR_DK, GLA_DV))

    y_s, tail_s = pl.pallas_call(
        _post_sample_kernel, grid=(1,),
        in_specs=[_const_spec(s) for s in ((Bs, Ls * O_W), (Bs, Ls * GATE_W), (Bs, Ls, D_MODEL),
                                           (Bs, CONV_WIDTH - 1, D_FF))] + post_w_specs,
        out_specs=[full(Bs, Ls, D_MODEL), full(Bs, CONV_WIDTH - 1, D_FF)],
        out_shape=[sds(Bs, Ls, D_MODEL), sds(Bs, CONV_WIDTH - 1, D_FF)],
        scratch_shapes=[pltpu.VMEM((CONV_WIDTH - 1, Bs, D_FF), F32)],
        compiler_params=_params(1), name="post_sample",
    )(o_s, gate_s, x_sample, cache_conv[0], *post_w)

    gshape = (1, -1, GLA_HEADS, GLA_DK, GLA_DV)
    return (y_p, y_s, sret_p[None], sret_s[None], sgla_p.reshape(gshape), sgla_s.reshape(gshape),
            tail_p[None], tail_s[None])
```

```python
import functools
import math

import jax
import jax.numpy as jnp
import numpy as np
from jax import lax
from jax.experimental import pallas as pl
from jax.experimental.pallas import tpu as pltpu

D_MODEL = 1024
PAST_LEN = 16384
RET_HEADS = 4
RET_DK = 128
RET_DV = 128
GLA_HEADS = 4
GLA_DK = 64
GLA_DV = 128
GLA_GATE_RANK = 16
GLA_GATE_TAU = 16.0
D_FF = 2816
CONV_WIDTH = 3
CHUNK = 64
RET_CHUNK = 128
ROPE_BASE = 10000.0
EPS = 1e-6

LANES = 128
SUBLANES = 8
VMEM_LIMIT = 56 * 1024 * 1024
RET_W = RET_HEADS * RET_DK
GLA_QK_W = GLA_HEADS * GLA_DK
GLA_V_W = GLA_HEADS * GLA_DV
GLA_PAIRS = GLA_HEADS // 2
PAIR_DK = 2 * GLA_DK
SCAN_W = 3 * RET_W + 2 * GLA_QK_W + GLA_V_W
GATE_W = RET_W + GLA_V_W + 2 * D_MODEL
O_W = RET_W + GLA_V_W
FFN_BLOCK = 256
MERGE_BLOCK = 256
GATE_BLOCK = 256
LOG_GAMMA = tuple(math.log(1.0 - 2.0 ** (-5.0 - h)) for h in range(RET_HEADS))
IN_COLS = tuple(np.cumsum((0, RET_W, RET_W, RET_W, RET_W, GLA_QK_W, GLA_QK_W, GLA_V_W, GLA_V_W,
                           GLA_GATE_RANK, D_MODEL, D_MODEL)).tolist())
MAIN_W = IN_COLS[8]

BF16 = jnp.bfloat16
F32 = jnp.float32


def _dot(a, b):
    return jnp.dot(a.astype(BF16), b.astype(BF16), preferred_element_type=F32)


def _dot_nt(a, b):
    return lax.dot_general(a, b, (((1,), (1,)), ((), ())), preferred_element_type=F32)


def _dot_tn(a, b):
    return lax.dot_general(a, b, (((0,), (0,)), ((), ())), preferred_element_type=F32)


def _split3(x):
    hi = x.astype(BF16)
    r = x - hi.astype(F32)
    mid = r.astype(BF16)
    lo = (r - mid.astype(F32)).astype(BF16)
    return hi, mid, lo


def _dot_exact_lhs01(m01, x):
    m = m01.astype(BF16)
    return sum(jnp.dot(m, t, preferred_element_type=F32) for t in _split3(x))


def _dot_tn_exact_rhs01(x, m01):
    m = m01.astype(BF16)
    return sum(_dot_tn(t, m) for t in _split3(x))


def _rms(x, g):
    return x * lax.rsqrt(jnp.mean(x * x, axis=-1, keepdims=True) + EPS) * g


def _rms_split(x, g):
    r = lax.rsqrt(jnp.mean(x * x, axis=-1, keepdims=True) + EPS)
    rb = jnp.broadcast_to(r, (x.shape[0], LANES))
    scale = lambda v: v * jnp.concatenate([rb] * (v.shape[1] // LANES), axis=1)
    return (x * g).astype(BF16), scale


def _sigmoid(x):
    return 0.5 * jnp.tanh(0.5 * x) + 0.5


def _proj_rows(x, cos, sin, ln1_ref, wmain_ref, wm_ref, wga_ref, wgu_ref, bgu_ref,
               put_scan, put_gate, put_la):
    n, row_scale = _rms_split(x, ln1_ref[...])
    mm = lambda ref, lo, hi: row_scale(jnp.dot(n, ref[:, lo:hi], preferred_element_type=F32))
    c = IN_COLS
    qk = mm(wmain_ref, c[0], c[2])
    ga = mm(wga_ref, 0, LANES)
    z = _dot(ga, wgu_ref[...]) + bgu_ref[...]
    log_sig = -(jnp.maximum(-z, 0.0) + jnp.log1p(jnp.exp(-jnp.abs(z))))
    put_la(0, log_sig / GLA_GATE_TAU)
    put_scan(2 * RET_W, mm(wmain_ref, c[2], c[3]))
    gqk = mm(wmain_ref, c[4], c[6])
    put_scan(3 * RET_W, gqk[:, :GLA_QK_W] * (GLA_DK ** -0.5))
    put_scan(3 * RET_W + GLA_QK_W, gqk[:, GLA_QK_W:])
    put_scan(3 * RET_W + 2 * GLA_QK_W, mm(wmain_ref, c[6], c[7]))
    for g in range(2 * RET_HEADS):
        lo = g * RET_DK
        t = qk[:, lo:lo + RET_DK]
        r = t * cos + pltpu.roll(t, RET_DK // 2, axis=1) * sin
        if g >= RET_HEADS:
            r = r * (RET_DK ** -0.5)
        put_scan(lo, r)
    jobs = []
    for ref, src, dst, width in ((wmain_ref, c[3], 0, RET_W), (wmain_ref, c[7], RET_W, GLA_V_W),
                                 (wm_ref, 0, O_W, 2 * D_MODEL)):
        for b in range(0, width, GATE_BLOCK):
            jobs.append(functools.partial(
                lambda ref, lo, dst: put_gate(dst, mm(ref, lo, lo + GATE_BLOCK)),
                ref, src + b, dst + b))
    return jobs


def _putter(ref, base=0):
    def put(lo, val):
        ref[:, base + lo:base + lo + val.shape[1]] = val
    return put


def _proj_sample_kernel(x_ref, cos_ref, sin_ref, *rest):
    w_refs, (scan_ref, gate_ref, la_ref) = rest[:6], rest[6:]
    for j in range(x_ref.shape[1]):
        for job in _proj_rows(x_ref[:, j, :], cos_ref[j:j + 1, :], sin_ref[j:j + 1, :], *w_refs,
                              _putter(scan_ref, j * SCAN_W), _putter(gate_ref, j * GATE_W),
                              _putter(la_ref, j * GLA_QK_W)):
            job()


N_PROJ_W = 6
N_POST_W = 9


def _layer_prompt_kernel(x_ref, cos_ref, sin_ref, *rest):
    proj_w, gn_ref = rest[:N_PROJ_W], rest[N_PROJ_W]
    post_w = rest[N_PROJ_W + 1:N_PROJ_W + 1 + N_POST_W]
    (y_ref, tail_ref, sret_ref, sgla_ref, qkv_ref, la_ref, o_ref, og_ref, gm_ref, act_ref,
     rhs_ret_ref, rhs_gla_ref, ubuf_ref) = rest[N_PROJ_W + 1 + N_POST_W:]

    @pl.when(pl.program_id(1) == 0)
    def _():
        sret_ref[...] = jnp.zeros_like(sret_ref)
        sgla_ref[...] = jnp.zeros_like(sgla_ref)
        ubuf_ref[0:SUBLANES, :] = jnp.zeros((SUBLANES, D_FF), F32)

    _attn_prompt_rows(x_ref, cos_ref, sin_ref, proj_w, gn_ref, act_ref, gm_ref, sret_ref, sgla_ref,
                      qkv_ref, la_ref, o_ref, og_ref, rhs_ret_ref, rhs_gla_ref)
    _post_prompt_rows(act_ref, gm_ref, x_ref, post_w, y_ref, tail_ref, ubuf_ref)


def _attn_prompt_rows(x_ref, cos_ref, sin_ref, w_refs, gn_ref, act_ref, gm_ref, sret_ref, sgla_ref,
                      qkv_ref, la_ref, o_ref, og_ref, rhs_ret_ref, rhs_gla_ref):
    def put_gate(lo, val):
        if lo < O_W:
            og_ref[:, lo:lo + val.shape[1]] = val
        else:
            gm_ref[:, lo - O_W:lo - O_W + val.shape[1]] = val

    jobs = _proj_rows(x_ref[...], cos_ref[...], sin_ref[...], *w_refs,
                      _putter(qkv_ref), put_gate, _putter(la_ref))

    T = x_ref.shape[0]
    n_heads = RET_HEADS + GLA_HEADS
    n_ticks = RET_HEADS * (T // RET_CHUNK) + GLA_PAIRS * (T // CHUNK) + 2
    n_scan_jobs = len(jobs) - (n_heads - 2)
    ticks = [0]

    def tick():
        ticks[0] += 1
        due = min(ticks[0] * n_scan_jobs // n_ticks, n_scan_jobs)
        while n_scan_jobs - (len(jobs) - (n_heads - 2)) < due:
            jobs.pop(0)()

    _scan_prompt_rows(qkv_ref, la_ref, o_ref, sret_ref, sgla_ref, rhs_ret_ref, rhs_gla_ref, tick)
    for h in range(n_heads):
        if jobs:
            jobs.pop(0)()
        act_ref[:, h * LANES:(h + 1) * LANES] = _head_act(o_ref, og_ref, gn_ref, h)
    assert not jobs


def _scan_prompt_rows(qkv_ref, la_ref, o_ref, sret_ref, sgla_ref, rhs_ret_ref, rhs_gla_ref, tick):
    T = qkv_ref.shape[0]
    n_rc = T // RET_CHUNK
    n_gc = T // CHUNK

    ri = lax.broadcasted_iota(jnp.int32, (RET_CHUNK, RET_CHUNK), 0)
    ci = lax.broadcasted_iota(jnp.int32, (RET_CHUNK, RET_CHUNK), 1)
    diff = (ri - ci).astype(F32)
    pos = (lax.broadcasted_iota(jnp.int32, (T, RET_DK), 0) % RET_CHUNK).astype(F32)
    ret_lhs, ret_upd = [], []
    for h in range(RET_HEADS):
        lg = LOG_GAMMA[h]
        q = qkv_ref[:, h * RET_DK:(h + 1) * RET_DK]
        k = qkv_ref[:, RET_W + h * RET_DK:RET_W + (h + 1) * RET_DK]
        vb = qkv_ref[:, 2 * RET_W + h * RET_DV:2 * RET_W + (h + 1) * RET_DV].astype(BF16)
        qb = q.astype(BF16)
        kb = k.astype(BF16)
        qd = (q * jnp.exp(lg * (pos + 1.0))).astype(BF16)
        kt = (k * jnp.exp(lg * (RET_CHUNK - 1.0 - pos))).astype(BF16)
        dm = jnp.where(ri >= ci, jnp.exp(lg * diff), 0.0)
        for c in range(n_rc):
            rc = slice(c * RET_CHUNK, (c + 1) * RET_CHUNK)
            rhs_ret_ref[c, h, RET_DK:, :] = vb[rc]
            ret_upd.append(_dot_tn(kt[rc], vb[rc]))
            s = _dot_nt(qb[rc], kb[rc]) * dm
            ret_lhs.append(jnp.concatenate([qd[rc], s.astype(BF16)], axis=1))
            tick()

    ti = lax.broadcasted_iota(jnp.int32, (T, T), 0)
    tj = lax.broadcasted_iota(jnp.int32, (T, T), 1)
    in_chunk_causal = ((ti // CHUNK) == (tj // CHUNK)) & (ti >= tj)
    go = 3 * RET_W
    bc = _dot_exact_lhs01(in_chunk_causal, la_ref[...])
    tick()
    bt = jnp.concatenate(
        [jnp.broadcast_to(bc[(c + 1) * CHUNK - 1:(c + 1) * CHUNK, :], (CHUNK, GLA_QK_W))
         for c in range(n_gc)], axis=0)
    gk = qkv_ref[:, go + GLA_QK_W:go + 2 * GLA_QK_W]
    qd_all = qkv_ref[:, go:go + GLA_QK_W] * jnp.exp(bc)
    kd_all = (gk * jnp.exp(-bc)).astype(BF16)
    kt_all = (gk * jnp.exp(bt - bc)).astype(BF16)
    sel = (lax.broadcasted_iota(jnp.int32, (T, LANES), 0)
           == lax.broadcasted_iota(jnp.int32, (T, LANES), 1) * CHUNK + (CHUNK - 1))
    bl = _dot_tn_exact_rhs01(bc, sel)
    tick()
    lane = lax.broadcasted_iota(jnp.int32, (1, PAIR_DK), 1)
    krow = lax.broadcasted_iota(jnp.int32, (PAIR_DK, 1), 0)
    si = lax.broadcasted_iota(jnp.int32, (CHUNK, 2 * CHUNK), 0)
    sj = lax.broadcasted_iota(jnp.int32, (CHUNK, 2 * CHUNK), 1)
    half_causal = [(sj // CHUNK == half) & (si >= sj % CHUNK) for half in range(2)]
    gla_lhs, gla_upd, gla_dec = [], [], []
    for p in range(GLA_PAIRS):
        cs = slice(p * PAIR_DK, (p + 1) * PAIR_DK)
        vo = go + 2 * GLA_QK_W + p * 2 * GLA_DV
        vp = qkv_ref[:, vo:vo + 2 * GLA_DV].astype(BF16)
        for c in range(n_gc):
            rc = slice(c * CHUNK, (c + 1) * CHUNK)
            rhs_gla_ref[c, p, PAIR_DK:PAIR_DK + CHUNK, :] = vp[rc, :GLA_DV]
            rhs_gla_ref[c, p, PAIR_DK + CHUNK:, :] = vp[rc, GLA_DV:]
            upd = _dot_tn(kt_all[rc, cs], vp[rc])
            gla_upd.append(jnp.where(krow < GLA_DK, upd[:, :GLA_DV], upd[:, GLA_DV:]))
            gla_dec.append(jnp.exp(jnp.broadcast_to(bl[cs, c:c + 1], (PAIR_DK, GLA_DV))))
            kk = jnp.concatenate([kd_all[rc, cs]] * 2, axis=0)
            blocks = []
            for half in range(2):
                qm = jnp.where((lane // GLA_DK) == half, qd_all[rc, cs], 0.0).astype(BF16)
                s = jnp.where(half_causal[half], _dot_nt(qm, kk), 0.0)
                blocks.append(jnp.concatenate([qm, s.astype(BF16)], axis=1))
            gla_lhs.append(jnp.concatenate(blocks, axis=0))
            tick()

    for h in range(RET_HEADS):
        S = sret_ref[h]
        for c in range(n_rc):
            rhs_ret_ref[c, h, :RET_DK, :] = S.astype(BF16)
            S = math.exp(LOG_GAMMA[h] * RET_CHUNK) * S + ret_upd[h * n_rc + c]
        sret_ref[h] = S
    for p in range(GLA_PAIRS):
        S = sgla_ref[p]
        for c in range(n_gc):
            rhs_gla_ref[c, p, :PAIR_DK, :] = S.astype(BF16)
            S = gla_dec[p * n_gc + c] * S + gla_upd[p * n_gc + c]
        sgla_ref[p] = S
    for h in range(RET_HEADS):
        for c in range(n_rc):
            o_ref[c * RET_CHUNK:(c + 1) * RET_CHUNK, h * RET_DV:(h + 1) * RET_DV] = jnp.dot(
                ret_lhs[h * n_rc + c], rhs_ret_ref[c, h], preferred_element_type=F32)
    for p in range(GLA_PAIRS):
        for c in range(n_gc):
            o2 = jnp.dot(gla_lhs[p * n_gc + c], rhs_gla_ref[c, p], preferred_element_type=F32)
            for half in range(2):
                co = RET_W + (2 * p + half) * GLA_DV
                o_ref[c * CHUNK:(c + 1) * CHUNK, co:co + GLA_DV] = o2[half * CHUNK:(half + 1) * CHUNK]


def _scan_sample_kernel(n_pos, qkv_ref, la_ref, sret_in_ref, sgla_in_ref, o_ref, sret_ref, sgla_ref):
    n_groups = qkv_ref.shape[0] // SUBLANES
    R = n_pos * SUBLANES
    seq_of_row = lax.broadcasted_iota(jnp.int32, (R, 1), 0) % SUBLANES
    seq_of_row2 = lax.broadcasted_iota(jnp.int32, (2 * R, 1), 0) % SUBLANES
    lane = lax.broadcasted_iota(jnp.int32, (1, PAIR_DK), 1)
    krow = lax.broadcasted_iota(jnp.int32, (PAIR_DK, 1), 0)
    eye = (lax.broadcasted_iota(jnp.int32, (SUBLANES, LANES), 0)
           == lax.broadcasted_iota(jnp.int32, (SUBLANES, LANES), 1))
    stack = lambda ts: jnp.concatenate(ts, axis=0)

    def body(g, carry):
        rows = pl.ds(pl.multiple_of(g * SUBLANES, SUBLANES), SUBLANES)
        col = lambda j, lo, w: qkv_ref[rows, j * SCAN_W + lo:j * SCAN_W + lo + w]

        for h in range(RET_HEADS):
            lg = LOG_GAMMA[h]
            q = [col(j, h * RET_DK, RET_DK) for j in range(n_pos)]
            k = [col(j, RET_W + h * RET_DK, RET_DK) for j in range(n_pos)]
            v = [col(j, 2 * RET_W + h * RET_DV, RET_DV) for j in range(n_pos)]
            intra = []
            for i in range(n_pos):
                acc = None
                for j in range(i + 1):
                    s = jnp.sum(q[i] * k[j], axis=-1, keepdims=True) * math.exp(lg * (i - j))
                    acc = s * v[j] if acc is None else acc + s * v[j]
                intra.append(acc)
            qd = stack([q[i] * math.exp(lg * (i + 1)) for i in range(n_pos)]).astype(BF16)
            kt = stack([k[j] * math.exp(lg * (n_pos - 1 - j)) for j in range(n_pos)])
            vs = stack(v).astype(BF16)
            inter = jnp.zeros((R, RET_DV), F32)
            for r in range(SUBLANES):
                mine = seq_of_row == r
                S = sret_in_ref[g * SUBLANES + r, h]
                inter = jnp.where(mine, jnp.dot(qd, S.astype(BF16), preferred_element_type=F32),
                                  inter)
                upd = _dot_tn(jnp.where(mine, kt, 0.0).astype(BF16), vs)
                sret_ref[g * SUBLANES + r, h] = math.exp(lg * n_pos) * S + upd
            for i in range(n_pos):
                o_ref[rows, i * O_W + h * RET_DV:i * O_W + (h + 1) * RET_DV] = (
                    intra[i] + inter[i * SUBLANES:(i + 1) * SUBLANES])

        go = 3 * RET_W
        for p in range(GLA_PAIRS):
            la = [la_ref[rows, j * GLA_QK_W + p * PAIR_DK:j * GLA_QK_W + (p + 1) * PAIR_DK]
                  for j in range(n_pos)]
            bc = [la[0]]
            for j in range(1, n_pos):
                bc.append(bc[-1] + la[j])
            bt = bc[-1]
            gq = [col(j, go + p * PAIR_DK, PAIR_DK) for j in range(n_pos)]
            gk = [col(j, go + GLA_QK_W + p * PAIR_DK, PAIR_DK) for j in range(n_pos)]
            vp = [col(j, go + 2 * GLA_QK_W + p * 2 * GLA_DV, 2 * GLA_DV) for j in range(n_pos)]
            qd = [gq[j] * jnp.exp(bc[j]) for j in range(n_pos)]
            kd = [gk[j] * jnp.exp(-bc[j]) for j in range(n_pos)]
            kt = stack([gk[j] * jnp.exp(bt - bc[j]) for j in range(n_pos)])
            halves = [(lane // GLA_DK) == half for half in range(2)]
            intra = [[None] * n_pos for _ in range(2)]
            for i in range(n_pos):
                for j in range(i + 1):
                    prod = qd[i] * kd[j]
                    for half in range(2):
                        s = jnp.sum(jnp.where(halves[half], prod, 0.0), axis=-1, keepdims=True)
                        term = s * vp[j][:, half * GLA_DV:(half + 1) * GLA_DV]
                        intra[half][i] = term if intra[half][i] is None else intra[half][i] + term
            qs = stack(qd)
            lhs = stack([jnp.where(halves[half], qs, 0.0) for half in range(2)]).astype(BF16)
            vs = stack(vp).astype(BF16)
            btt = _dot_tn_exact_rhs01(bt, eye)
            inter = jnp.zeros((2 * R, GLA_DV), F32)
            for r in range(SUBLANES):
                S = sgla_in_ref[g * SUBLANES + r, p]
                inter = jnp.where(seq_of_row2 == r,
                                  jnp.dot(lhs, S.astype(BF16), preferred_element_type=F32), inter)
                upd = _dot_tn(jnp.where(seq_of_row == r, kt, 0.0).astype(BF16), vs)
                upd = jnp.where(krow < GLA_DK, upd[:, :GLA_DV], upd[:, GLA_DV:])
                dec = jnp.exp(jnp.broadcast_to(btt[:, r:r + 1], (PAIR_DK, GLA_DV)))
                sgla_ref[g * SUBLANES + r, p] = dec * S + upd
            for half in range(2):
                for i in range(n_pos):
                    co = i * O_W + RET_W + (2 * p + half) * GLA_DV
                    lo = half * R + i * SUBLANES
                    o_ref[rows, co:co + GLA_DV] = intra[half][i] + inter[lo:lo + SUBLANES]
        return carry

    lax.fori_loop(0, n_groups, body, 0)


def _head_act(o_ref, gate_ref, gn_ref, h):
    cs = slice(h * LANES, (h + 1) * LANES)
    oh = o_ref[:, cs]
    yh = oh * lax.rsqrt(jnp.mean(oh * oh, axis=-1, keepdims=True) + EPS) * gn_ref[:, cs]
    gh = gate_ref[:, cs]
    return (yh * (gh * _sigmoid(gh))).astype(BF16)


def _post_rows(a_r, a_g, gm_ref, x, wro_ref, wgo_ref, wo_ref, ln2_ref, wup_ref, cw_ref,
               cb_ref, wdn_ref, lnf_ref, conv_taps):
    blocks = [slice(lo, lo + MERGE_BLOCK) for lo in range(0, D_MODEL, MERGE_BLOCK)]

    def out_block(cs):
        return (jnp.dot(a_r, wro_ref[:, cs], preferred_element_type=F32),
                jnp.dot(a_g, wgo_ref[:, cs], preferred_element_type=F32))

    mix = []
    y_next = out_block(blocks[0])
    for i, cs in enumerate(blocks):
        y_r, y_g = y_next
        if i + 1 < len(blocks):
            y_next = out_block(blocks[i + 1])
        m_r = gm_ref[:, cs]
        m_g = gm_ref[:, D_MODEL + cs.start:D_MODEL + cs.stop]
        mix.append((_sigmoid(m_r) * y_r + _sigmoid(m_g) * y_g).astype(BF16))
    mix = jnp.concatenate(mix, axis=1)
    h1 = jnp.concatenate(
        [x[:, cs] + jnp.dot(mix, wo_ref[:, cs], preferred_element_type=F32) for cs in blocks],
        axis=1)
    n2, row_scale = _rms_split(h1, ln2_ref[...])

    nb = D_FF // FFN_BLOCK

    def up_block(j):
        lo = j * FFN_BLOCK
        return (jnp.dot(n2, wup_ref[:, lo:lo + FFN_BLOCK], preferred_element_type=F32),
                jnp.dot(n2, wup_ref[:, D_FF + lo:D_FF + lo + FFN_BLOCK],
                        preferred_element_type=F32))

    h2 = h1
    uv_next = up_block(0)
    for j in range(nb):
        cs = slice(j * FFN_BLOCK, (j + 1) * FFN_BLOCK)
        u, vv = (row_scale(t) for t in uv_next)
        if j + 1 < nb:
            uv_next = up_block(j + 1)
        u_m2, u_m1 = conv_taps(cs, u)
        uc = cb_ref[:, cs] + u_m2 * cw_ref[0:1, cs]
        uc = uc + u_m1 * cw_ref[1:2, cs]
        uc = uc + u * cw_ref[2:3, cs]
        act = 0.5 * uc * (1.0 + lax.erf(uc * float(np.float32(np.sqrt(0.5))))) * vv
        h2 = h2 + _dot(act, wdn_ref[cs, :])
    return _rms(h2, lnf_ref[...])


def _post_prompt_rows(act_ref, gm_ref, x_ref, w_refs, y_ref, tail_ref, ubuf_ref):
    T = x_ref.shape[0]

    def conv_taps(cs, u):
        ubuf_ref[SUBLANES:SUBLANES + T, cs] = u
        return (ubuf_ref[SUBLANES - 2:SUBLANES - 2 + T, cs],
                ubuf_ref[SUBLANES - 1:SUBLANES - 1 + T, cs])

    y_ref[...] = _post_rows(act_ref[:, :RET_W], act_ref[:, RET_W:], gm_ref, x_ref[...], *w_refs,
                            conv_taps)
    tail_ref[...] = ubuf_ref[T + SUBLANES - (CONV_WIDTH - 1):T + SUBLANES, :]
    ubuf_ref[0:SUBLANES, :] = ubuf_ref[T:T + SUBLANES, :]


def _post_sample_kernel(o_ref, gate_ref, x_ref, cache_ref, *rest):
    w_refs, (y_ref, tail_ref, ubuf_ref) = rest[:10], rest[10:]
    for k in range(CONV_WIDTH - 1):
        ubuf_ref[k] = cache_ref[:, k, :]

    def conv_taps(cs, u):
        u_m2 = ubuf_ref[0, :, cs]
        u_m1 = ubuf_ref[1, :, cs]
        ubuf_ref[0, :, cs] = u_m1
        ubuf_ref[1, :, cs] = u
        return u_m2, u_m1

    gn_ref, w_refs = w_refs[0], w_refs[1:]
    for j in range(x_ref.shape[1]):
        o_j = o_ref.at[:, j * O_W:(j + 1) * O_W]
        gate_j = gate_ref.at[:, j * GATE_W:(j + 1) * GATE_W]
        acts = [_head_act(o_j, gate_j, gn_ref, h) for h in range(RET_HEADS + GLA_HEADS)]
        y_ref[:, j, :] = _post_rows(jnp.concatenate(acts[:RET_HEADS], axis=1),
                                    jnp.concatenate(acts[RET_HEADS:], axis=1),
                                    gate_j.at[:, O_W:], x_ref[:, j, :], *w_refs, conv_taps)
    for k in range(CONV_WIDTH - 1):
        tail_ref[:, k, :] = ubuf_ref[k]


def _split_w_in_kernel(w_ref, main_ref, m_ref, ga_ref):
    c = IN_COLS
    main_ref[...] = w_ref[:, :MAIN_W].astype(BF16)
    m_ref[...] = w_ref[:, c[9]:].astype(BF16)
    lane = lax.broadcasted_iota(jnp.int32, (1, LANES), 1)
    ga_ref[...] = jnp.where(lane < GLA_GATE_RANK, w_ref[:, c[8]:c[8] + LANES], 0.0).astype(BF16)


def _const_spec(shape):
    nd = len(shape)
    return pl.BlockSpec(shape, lambda *_: (0,) * nd, pipeline_mode=pl.Buffered(1))


def _params(n_grid):
    return pltpu.CompilerParams(dimension_semantics=("arbitrary",) * n_grid,
                                vmem_limit_bytes=VMEM_LIMIT)


def _rope_tables(pos):
    half = RET_DK // 2
    inv = ROPE_BASE ** (-jnp.arange(half, dtype=F32) / half)
    ang = pos[:, None] * inv[None, :]
    cos, sin = jnp.cos(ang), jnp.sin(ang)
    return jnp.concatenate([cos, cos], axis=-1), jnp.concatenate([-sin, sin], axis=-1)


def kernel(x_prompt, x_sample, state_ret, state_gla, cache_conv, ln1, w_in, w_gate_up, b_gate_up,
           g_ret, g_gla, w_ret_out, w_gla_out, w_o, ln2, w_up, conv_w, conv_b, w_down, ln_f):
    Bp, Lp, _ = x_prompt.shape
    Bs, Ls, _ = x_sample.shape
    assert state_ret.shape[0] == 1, "single layer"
    assert Ls >= CONV_WIDTH - 1 and Bs % SUBLANES == 0

    WB = 128
    w_main, w_m, w_ga = pl.pallas_call(
        _split_w_in_kernel, grid=(D_MODEL // WB,),
        in_specs=[pl.BlockSpec((None, WB, IN_COLS[-1]), lambda i: (0, i, 0))],
        out_specs=[pl.BlockSpec((WB, w), lambda i: (i, 0)) for w in (MAIN_W, 2 * D_MODEL, LANES)],
        out_shape=[jax.ShapeDtypeStruct((D_MODEL, w), BF16)
                   for w in (MAIN_W, 2 * D_MODEL, LANES)],
        compiler_params=_params(1), name="split_w_in",
    )(w_in)
    proj_w = [ln1[0][None, :], w_main, w_m, w_ga,
              jnp.pad(w_gate_up[0], ((0, LANES - GLA_GATE_RANK), (0, 0))).astype(BF16),
              b_gate_up[0][None, :]]
    post_w = [jnp.concatenate([g_ret[0], g_gla[0]])[None, :], w_ret_out[0].astype(BF16),
              w_gla_out[0].astype(BF16), w_o[0].astype(BF16), ln2[0][None, :],
              w_up[0].astype(BF16), conv_w[0], conv_b[0][None, :], w_down[0].astype(BF16),
              ln_f[None, :]]
    proj_w_specs = [_const_spec(w.shape) for w in proj_w]
    post_w_specs = [_const_spec(w.shape) for w in post_w]
    cos_p, sin_p = _rope_tables(jnp.arange(Lp, dtype=F32))
    cos_s, sin_s = _rope_tables(PAST_LEN + jnp.arange(Ls, dtype=F32))
    sds = lambda *shape: jax.ShapeDtypeStruct(shape, F32)
    full = lambda *shape: pl.BlockSpec(shape, lambda *_: (0,) * len(shape))

    T = 256
    grid_p = (Bp, Lp // T)
    rows_p = lambda w: pl.BlockSpec((None, T, w), lambda b, t: (b, t, 0))
    state_p = lambda *shape: pl.BlockSpec((None,) + shape, lambda b, t: (b,) + (0,) * len(shape))
    rope_spec = pl.BlockSpec((T, RET_DK), lambda b, t: (t, 0))
    y_p, tail_p, sret_p, sgla_p = pl.pallas_call(
        _layer_prompt_kernel, grid=grid_p,
        in_specs=[rows_p(D_MODEL), rope_spec, rope_spec] + proj_w_specs + post_w_specs,
        out_specs=[rows_p(D_MODEL), state_p(CONV_WIDTH - 1, D_FF),
                   state_p(RET_HEADS, RET_DK, RET_DV), state_p(GLA_PAIRS, PAIR_DK, GLA_DV)],
        out_shape=[sds(Bp, Lp, D_MODEL), sds(Bp, CONV_WIDTH - 1, D_FF),
                   sds(Bp, RET_HEADS, RET_DK, RET_DV), sds(Bp, GLA_PAIRS, PAIR_DK, GLA_DV)],
        scratch_shapes=[
            pltpu.VMEM((T, SCAN_W), F32), pltpu.VMEM((T, GLA_QK_W), F32),
            pltpu.VMEM((T, O_W), F32), pltpu.VMEM((T, O_W), F32),
            pltpu.VMEM((T, 2 * D_MODEL), F32), pltpu.VMEM((T, O_W), BF16),
            pltpu.VMEM((T // RET_CHUNK, RET_HEADS, RET_DK + RET_CHUNK, RET_DV), BF16),
            pltpu.VMEM((T // CHUNK, GLA_PAIRS, PAIR_DK + 2 * CHUNK, GLA_DV), BF16),
            pltpu.VMEM((T + SUBLANES, D_FF), F32)],
        compiler_params=_params(2), name="layer_prompt",
    )(x_prompt, cos_p, sin_p, *proj_w, *post_w)

    scan_s, gate_s, la_s = pl.pallas_call(
        _proj_sample_kernel, grid=(1,),
        in_specs=[_const_spec((Bs, Ls, D_MODEL)), full(Ls, RET_DK), full(Ls, RET_DK)]
        + proj_w_specs,
        out_specs=[full(Bs, Ls * SCAN_W), full(Bs, Ls * GATE_W), full(Bs, Ls * GLA_QK_W)],
        out_shape=[sds(Bs, Ls * SCAN_W), sds(Bs, Ls * GATE_W), sds(Bs, Ls * GLA_QK_W)],
        compiler_params=_params(1), name="proj_sample",
    )(x_sample, cos_s, sin_s, *proj_w)

    SB = 16
    rows_s = lambda w: pl.BlockSpec((SB, w), lambda i: (i, 0))
    state_s = lambda *shape: pl.BlockSpec((SB,) + shape, lambda i: (i,) + (0,) * len(shape))
    o_s, sret_s, sgla_s = pl.pallas_call(
        functools.partial(_scan_sample_kernel, Ls), grid=(Bs // SB,),
        in_specs=[rows_s(Ls * SCAN_W), rows_s(Ls * GLA_QK_W),
                  state_s(RET_HEADS, RET_DK, RET_DV), state_s(GLA_PAIRS, PAIR_DK, GLA_DV)],
        out_specs=[rows_s(Ls * O_W), state_s(RET_HEADS, RET_DK, RET_DV),
                   state_s(GLA_PAIRS, PAIR_DK, GLA_DV)],
        out_shape=[sds(Bs, Ls * O_W), sds(Bs, RET_HEADS, RET_DK, RET_DV),
                   sds(Bs, GLA_PAIRS, PAIR_DK, GLA_DV)],
        compiler_params=_params(1), name="scan_sample",
    )(scan_s, la_s, state_ret[0], state_gla[0].reshape(Bs, GLA_PAIRS, PAIR_DK, GLA_DV))

    y_s, tail_s = pl.pallas_call(
        _post_sample_kernel, grid=(1,),
        in_specs=[_const_spec(s) for s in ((Bs, Ls * O_W), (Bs, Ls * GATE_W), (Bs, Ls, D_MODEL),
                                           (Bs, CONV_WIDTH - 1, D_FF))] + post_w_specs,
        out_specs=[full(Bs, Ls, D_MODEL), full(Bs, CONV_WIDTH - 1, D_FF)],
        out_shape=[sds(Bs, Ls, D_MODEL), sds(Bs, CONV_WIDTH - 1, D_FF)],
        scratch_shapes=[pltpu.VMEM((CONV_WIDTH - 1, Bs, D_FF), F32)],
        compiler_params=_params(1), name="post_sample",
    )(o_s, gate_s, x_sample, cache_conv[0], *post_w)

    gshape = (1, -1, GLA_HEADS, GLA_DK, GLA_DV)
    return (y_p, y_s, sret_p[None], sret_s[None], sgla_p.reshape(gshape), sgla_s.reshape(gshape),
            tail_p[None], tail_s[None])
```

```python
import functools
import math

import jax
import jax.numpy as jnp
import numpy as np
from jax import lax
from jax.experimental import pallas as pl
from jax.experimental.pallas import tpu as pltpu

D_MODEL = 1024
PAST_LEN = 16384
RET_HEADS = 4
RET_DK = 128
RET_DV = 128
GLA_HEADS = 4
GLA_DK = 64
GLA_DV = 128
GLA_GATE_RANK = 16
GLA_GATE_TAU = 16.0
D_FF = 2816
CONV_WIDTH = 3
CHUNK = 64
RET_CHUNK = 128
ROPE_BASE = 10000.0
EPS = 1e-6

LANES = 128
SUBLANES = 8
VMEM_LIMIT = 56 * 1024 * 1024
RET_W = RET_HEADS * RET_DK
GLA_QK_W = GLA_HEADS * GLA_DK
GLA_V_W = GLA_HEADS * GLA_DV
GLA_PAIRS = GLA_HEADS // 2
PAIR_DK = 2 * GLA_DK
SCAN_W = 3 * RET_W + 2 * GLA_QK_W + GLA_V_W
GATE_W = RET_W + GLA_V_W + 2 * D_MODEL
O_W = RET_W + GLA_V_W
FFN_BLOCK = 256
MERGE_BLOCK = 256
GATE_BLOCK = 256
LOG_GAMMA = tuple(math.log(1.0 - 2.0 ** (-5.0 - h)) for h in range(RET_HEADS))
IN_COLS = tuple(np.cumsum((0, RET_W, RET_W, RET_W, RET_W, GLA_QK_W, GLA_QK_W, GLA_V_W, GLA_V_W,
                           GLA_GATE_RANK, D_MODEL, D_MODEL)).tolist())
MAIN_W = IN_COLS[8]

BF16 = jnp.bfloat16
F32 = jnp.float32


def _dot(a, b):
    return jnp.dot(a.astype(BF16), b.astype(BF16), preferred_element_type=F32)


def _dot_nt(a, b):
    return lax.dot_general(a, b, (((1,), (1,)), ((), ())), preferred_element_type=F32)


def _dot_tn(a, b):
    return lax.dot_general(a, b, (((0,), (0,)), ((), ())), preferred_element_type=F32)


def _split3(x):
    hi = x.astype(BF16)
    r = x - hi.astype(F32)
    mid = r.astype(BF16)
    lo = (r - mid.astype(F32)).astype(BF16)
    return hi, mid, lo


def _dot_exact_lhs01(m01, x):
    m = m01.astype(BF16)
    return sum(jnp.dot(m, t, preferred_element_type=F32) for t in _split3(x))


def _dot_tn_exact_rhs01(x, m01):
    m = m01.astype(BF16)
    return sum(_dot_tn(t, m) for t in _split3(x))


def _rms(x, g):
    return x * lax.rsqrt(jnp.mean(x * x, axis=-1, keepdims=True) + EPS) * g


def _rms_split(x, g):
    r = lax.rsqrt(jnp.mean(x * x, axis=-1, keepdims=True) + EPS)
    rb = jnp.broadcast_to(r, (x.shape[0], LANES))
    scale = lambda v: v * jnp.concatenate([rb] * (v.shape[1] // LANES), axis=1)
    return (x * g).astype(BF16), scale


def _sigmoid(x):
    return 0.5 * jnp.tanh(0.5 * x) + 0.5


def _proj_rows(x, cos, sin, ln1_ref, wmain_ref, wm_ref, wga_ref, wgu_ref, bgu_ref,
               put_scan, put_gate, put_la):
    n, row_scale = _rms_split(x, ln1_ref[...])
    mm = lambda ref, lo, hi: row_scale(jnp.dot(n, ref[:, lo:hi], preferred_element_type=F32))
    c = IN_COLS
    qk = mm(wmain_ref, c[0], c[2])
    ga = mm(wga_ref, 0, LANES)
    z = _dot(ga, wgu_ref[...]) + bgu_ref[...]
    log_sig = -(jnp.maximum(-z, 0.0) + jnp.log1p(jnp.exp(-jnp.abs(z))))
    put_la(0, log_sig / GLA_GATE_TAU)
    put_scan(2 * RET_W, mm(wmain_ref, c[2], c[3]))
    gqk = mm(wmain_ref, c[4], c[6])
    put_scan(3 * RET_W, gqk[:, :GLA_QK_W] * (GLA_DK ** -0.5))
    put_scan(3 * RET_W + GLA_QK_W, gqk[:, GLA_QK_W:])
    put_scan(3 * RET_W + 2 * GLA_QK_W, mm(wmain_ref, c[6], c[7]))
    for g in range(2 * RET_HEADS):
        lo = g * RET_DK
        t = qk[:, lo:lo + RET_DK]
        r = t * cos + pltpu.roll(t, RET_DK // 2, axis=1) * sin
        if g >= RET_HEADS:
            r = r * (RET_DK ** -0.5)
        put_scan(lo, r)
    jobs = []
    for ref, src, dst, width in ((wmain_ref, c[3], 0, RET_W), (wmain_ref, c[7], RET_W, GLA_V_W),
                                 (wm_ref, 0, O_W, 2 * D_MODEL)):
        for b in range(0, width, GATE_BLOCK):
            jobs.append(functools.partial(
                lambda ref, lo, dst: put_gate(dst, mm(ref, lo, lo + GATE_BLOCK)),
                ref, src + b, dst + b))
    return jobs


def _putter(ref, base=0):
    def put(lo, val):
        ref[:, base + lo:base + lo + val.shape[1]] = val
    return put


def _proj_sample_kernel(x_ref, cos_ref, sin_ref, *rest):
    w_refs, (scan_ref, gate_ref, la_ref) = rest[:6], rest[6:]
    n_seq, n_pos, _ = x_ref.shape
    by_pos = lambda f: jnp.concatenate([f(j) for j in range(n_pos)], axis=0)
    table = lambda ref: by_pos(lambda j: jnp.broadcast_to(ref[j:j + 1, :], (n_seq, RET_DK)))

    def pos_putter(ref, width):
        def put(lo, val):
            for j in range(n_pos):
                ref[:, j * width + lo:j * width + lo + val.shape[1]] = (
                    val[j * n_seq:(j + 1) * n_seq])
        return put

    for job in _proj_rows(by_pos(lambda j: x_ref[:, j, :]), table(cos_ref), table(sin_ref), *w_refs,
                          pos_putter(scan_ref, SCAN_W), pos_putter(gate_ref, GATE_W),
                          pos_putter(la_ref, GLA_QK_W)):
        job()


N_PROJ_W = 6
N_POST_W = 9


def _layer_prompt_kernel(x_ref, cos_ref, sin_ref, *rest):
    proj_w, gn_ref = rest[:N_PROJ_W], rest[N_PROJ_W]
    post_w = rest[N_PROJ_W + 1:N_PROJ_W + 1 + N_POST_W]
    (y_ref, tail_ref, sret_ref, sgla_ref, qkv_ref, la_ref, o_ref, og_ref, gm_ref, act_ref,
     rhs_ret_ref, rhs_gla_ref, ubuf_ref) = rest[N_PROJ_W + 1 + N_POST_W:]

    @pl.when(pl.program_id(1) == 0)
    def _():
        sret_ref[...] = jnp.zeros_like(sret_ref)
        sgla_ref[...] = jnp.zeros_like(sgla_ref)
        ubuf_ref[0:SUBLANES, :] = jnp.zeros((SUBLANES, D_FF), F32)

    _attn_prompt_rows(x_ref, cos_ref, sin_ref, proj_w, gn_ref, act_ref, gm_ref, sret_ref, sgla_ref,
                      qkv_ref, la_ref, o_ref, og_ref, rhs_ret_ref, rhs_gla_ref)
    _post_prompt_rows(act_ref, gm_ref, x_ref, post_w, y_ref, tail_ref, ubuf_ref)


def _attn_prompt_rows(x_ref, cos_ref, sin_ref, w_refs, gn_ref, act_ref, gm_ref, sret_ref, sgla_ref,
                      qkv_ref, la_ref, o_ref, og_ref, rhs_ret_ref, rhs_gla_ref):
    def put_gate(lo, val):
        if lo < O_W:
            og_ref[:, lo:lo + val.shape[1]] = val
        else:
            gm_ref[:, lo - O_W:lo - O_W + val.shape[1]] = val

    jobs = _proj_rows(x_ref[...], cos_ref[...], sin_ref[...], *w_refs,
                      _putter(qkv_ref), put_gate, _putter(la_ref))

    T = x_ref.shape[0]
    n_heads = RET_HEADS + GLA_HEADS
    n_ticks = RET_HEADS * (T // RET_CHUNK) + GLA_PAIRS * (T // CHUNK) + 2
    n_scan_jobs = len(jobs) - (n_heads - 2)
    ticks = [0]

    def tick():
        ticks[0] += 1
        due = min(ticks[0] * n_scan_jobs // n_ticks, n_scan_jobs)
        while n_scan_jobs - (len(jobs) - (n_heads - 2)) < due:
            jobs.pop(0)()

    _scan_prompt_rows(qkv_ref, la_ref, o_ref, sret_ref, sgla_ref, rhs_ret_ref, rhs_gla_ref, tick)
    for h in range(n_heads):
        if jobs:
            jobs.pop(0)()
        act_ref[:, h * LANES:(h + 1) * LANES] = _head_act(o_ref, og_ref, gn_ref, h)
    assert not jobs


def _scan_prompt_rows(qkv_ref, la_ref, o_ref, sret_ref, sgla_ref, rhs_ret_ref, rhs_gla_ref, tick):
    T = qkv_ref.shape[0]
    n_rc = T // RET_CHUNK
    n_gc = T // CHUNK

    ri = lax.broadcasted_iota(jnp.int32, (RET_CHUNK, RET_CHUNK), 0)
    ci = lax.broadcasted_iota(jnp.int32, (RET_CHUNK, RET_CHUNK), 1)
    diff = (ri - ci).astype(F32)
    pos = (lax.broadcasted_iota(jnp.int32, (T, RET_DK), 0) % RET_CHUNK).astype(F32)
    ret_lhs, ret_upd = [], []
    for h in range(RET_HEADS):
        lg = LOG_GAMMA[h]
        q = qkv_ref[:, h * RET_DK:(h + 1) * RET_DK]
        k = qkv_ref[:, RET_W + h * RET_DK:RET_W + (h + 1) * RET_DK]
        vb = qkv_ref[:, 2 * RET_W + h * RET_DV:2 * RET_W + (h + 1) * RET_DV].astype(BF16)
        qb = q.astype(BF16)
        kb = k.astype(BF16)
        qd = (q * jnp.exp(lg * (pos + 1.0))).astype(BF16)
        kt = (k * jnp.exp(lg * (RET_CHUNK - 1.0 - pos))).astype(BF16)
        dm = jnp.where(ri >= ci, jnp.exp(lg * diff), 0.0)
        for c in range(n_rc):
            rc = slice(c * RET_CHUNK, (c + 1) * RET_CHUNK)
            rhs_ret_ref[c, h, RET_DK:, :] = vb[rc]
            ret_upd.append(_dot_tn(kt[rc], vb[rc]))
            s = _dot_nt(qb[rc], kb[rc]) * dm
            ret_lhs.append(jnp.concatenate([qd[rc], s.astype(BF16)], axis=1))
            tick()

    ti = lax.broadcasted_iota(jnp.int32, (T, T), 0)
    tj = lax.broadcasted_iota(jnp.int32, (T, T), 1)
    in_chunk_causal = ((ti // CHUNK) == (tj // CHUNK)) & (ti >= tj)
    go = 3 * RET_W
    bc = _dot_exact_lhs01(in_chunk_causal, la_ref[...])
    tick()
    bt = jnp.concatenate(
        [jnp.broadcast_to(bc[(c + 1) * CHUNK - 1:(c + 1) * CHUNK, :], (CHUNK, GLA_QK_W))
         for c in range(n_gc)], axis=0)
    gk = qkv_ref[:, go + GLA_QK_W:go + 2 * GLA_QK_W]
    qd_all = qkv_ref[:, go:go + GLA_QK_W] * jnp.exp(bc)
    kd_all = (gk * jnp.exp(-bc)).astype(BF16)
    kt_all = (gk * jnp.exp(bt - bc)).astype(BF16)
    sel = (lax.broadcasted_iota(jnp.int32, (T, LANES), 0)
           == lax.broadcasted_iota(jnp.int32, (T, LANES), 1) * CHUNK + (CHUNK - 1))
    bl = _dot_tn_exact_rhs01(bc, sel)
    tick()
    lane = lax.broadcasted_iota(jnp.int32, (1, PAIR_DK), 1)
    krow = lax.broadcasted_iota(jnp.int32, (PAIR_DK, 1), 0)
    si = lax.broadcasted_iota(jnp.int32, (CHUNK, 2 * CHUNK), 0)
    sj = lax.broadcasted_iota(jnp.int32, (CHUNK, 2 * CHUNK), 1)
    half_causal = [(sj // CHUNK == half) & (si >= sj % CHUNK) for half in range(2)]
    gla_lhs, gla_upd, gla_dec = [], [], []
    for p in range(GLA_PAIRS):
        cs = slice(p * PAIR_DK, (p + 1) * PAIR_DK)
        vo = go + 2 * GLA_QK_W + p * 2 * GLA_DV
        vp = qkv_ref[:, vo:vo + 2 * GLA_DV].astype(BF16)
        for c in range(n_gc):
            rc = slice(c * CHUNK, (c + 1) * CHUNK)
            rhs_gla_ref[c, p, PAIR_DK:PAIR_DK + CHUNK, :] = vp[rc, :GLA_DV]
            rhs_gla_ref[c, p, PAIR_DK + CHUNK:, :] = vp[rc, GLA_DV:]
            upd = _dot_tn(kt_all[rc, cs], vp[rc])
            gla_upd.append(jnp.where(krow < GLA_DK, upd[:, :GLA_DV], upd[:, GLA_DV:]))
            gla_dec.append(jnp.exp(jnp.broadcast_to(bl[cs, c:c + 1], (PAIR_DK, GLA_DV))))
            kk = jnp.concatenate([kd_all[rc, cs]] * 2, axis=0)
            blocks = []
            for half in range(2):
                qm = jnp.where((lane // GLA_DK) == half, qd_all[rc, cs], 0.0).astype(BF16)
                s = jnp.where(half_causal[half], _dot_nt(qm, kk), 0.0)
                blocks.append(jnp.concatenate([qm, s.astype(BF16)], axis=1))
            gla_lhs.append(jnp.concatenate(blocks, axis=0))
            tick()

    for h in range(RET_HEADS):
        S = sret_ref[h]
        for c in range(n_rc):
            rhs_ret_ref[c, h, :RET_DK, :] = S.astype(BF16)
            S = math.exp(LOG_GAMMA[h] * RET_CHUNK) * S + ret_upd[h * n_rc + c]
        sret_ref[h] = S
    for p in range(GLA_PAIRS):
        S = sgla_ref[p]
        for c in range(n_gc):
            rhs_gla_ref[c, p, :PAIR_DK, :] = S.astype(BF16)
            S = gla_dec[p * n_gc + c] * S + gla_upd[p * n_gc + c]
        sgla_ref[p] = S
    for h in range(RET_HEADS):
        for c in range(n_rc):
            o_ref[c * RET_CHUNK:(c + 1) * RET_CHUNK, h * RET_DV:(h + 1) * RET_DV] = jnp.dot(
                ret_lhs[h * n_rc + c], rhs_ret_ref[c, h], preferred_element_type=F32)
    for p in range(GLA_PAIRS):
        for c in range(n_gc):
            o2 = jnp.dot(gla_lhs[p * n_gc + c], rhs_gla_ref[c, p], preferred_element_type=F32)
            for half in range(2):
                co = RET_W + (2 * p + half) * GLA_DV
                o_ref[c * CHUNK:(c + 1) * CHUNK, co:co + GLA_DV] = o2[half * CHUNK:(half + 1) * CHUNK]


def _scan_sample_kernel(n_pos, qkv_ref, la_ref, sret_in_ref, sgla_in_ref, o_ref, sret_ref, sgla_ref):
    n_groups = qkv_ref.shape[0] // SUBLANES
    R = n_pos * SUBLANES
    seq_of_row = lax.broadcasted_iota(jnp.int32, (R, 1), 0) % SUBLANES
    seq_of_row2 = lax.broadcasted_iota(jnp.int32, (2 * R, 1), 0) % SUBLANES
    lane = lax.broadcasted_iota(jnp.int32, (1, PAIR_DK), 1)
    krow = lax.broadcasted_iota(jnp.int32, (PAIR_DK, 1), 0)
    eye = (lax.broadcasted_iota(jnp.int32, (SUBLANES, LANES), 0)
           == lax.broadcasted_iota(jnp.int32, (SUBLANES, LANES), 1))
    stack = lambda ts: jnp.concatenate(ts, axis=0)

    def body(g, carry):
        rows = pl.ds(pl.multiple_of(g * SUBLANES, SUBLANES), SUBLANES)
        col = lambda j, lo, w: qkv_ref[rows, j * SCAN_W + lo:j * SCAN_W + lo + w]

        for h in range(RET_HEADS):
            lg = LOG_GAMMA[h]
            q = [col(j, h * RET_DK, RET_DK) for j in range(n_pos)]
            k = [col(j, RET_W + h * RET_DK, RET_DK) for j in range(n_pos)]
            v = [col(j, 2 * RET_W + h * RET_DV, RET_DV) for j in range(n_pos)]
            intra = []
            for i in range(n_pos):
                acc = None
                for j in range(i + 1):
                    s = jnp.sum(q[i] * k[j], axis=-1, keepdims=True) * math.exp(lg * (i - j))
                    acc = s * v[j] if acc is None else acc + s * v[j]
                intra.append(acc)
            qd = stack([q[i] * math.exp(lg * (i + 1)) for i in range(n_pos)]).astype(BF16)
            kt = stack([k[j] * math.exp(lg * (n_pos - 1 - j)) for j in range(n_pos)])
            vs = stack(v).astype(BF16)
            inter = jnp.zeros((R, RET_DV), F32)
            for r in range(SUBLANES):
                mine = seq_of_row == r
                S = sret_in_ref[g * SUBLANES + r, h]
                inter = jnp.where(mine, jnp.dot(qd, S.astype(BF16), preferred_element_type=F32),
                                  inter)
                upd = _dot_tn(jnp.where(mine, kt, 0.0).astype(BF16), vs)
                sret_ref[g * SUBLANES + r, h] = math.exp(lg * n_pos) * S + upd
            for i in range(n_pos):
                o_ref[rows, i * O_W + h * RET_DV:i * O_W + (h + 1) * RET_DV] = (
                    intra[i] + inter[i * SUBLANES:(i + 1) * SUBLANES])

        go = 3 * RET_W
        for p in range(GLA_PAIRS):
            la = [la_ref[rows, j * GLA_QK_W + p * PAIR_DK:j * GLA_QK_W + (p + 1) * PAIR_DK]
                  for j in range(n_pos)]
            bc = [la[0]]
            for j in range(1, n_pos):
                bc.append(bc[-1] + la[j])
            bt = bc[-1]
            gq = [col(j, go + p * PAIR_DK, PAIR_DK) for j in range(n_pos)]
            gk = [col(j, go + GLA_QK_W + p * PAIR_DK, PAIR_DK) for j in range(n_pos)]
            vp = [col(j, go + 2 * GLA_QK_W + p * 2 * GLA_DV, 2 * GLA_DV) for j in range(n_pos)]
            qd = [gq[j] * jnp.exp(bc[j]) for j in range(n_pos)]
            kd = [gk[j] * jnp.exp(-bc[j]) for j in range(n_pos)]
            kt = stack([gk[j] * jnp.exp(bt - bc[j]) for j in range(n_pos)])
            halves = [(lane // GLA_DK) == half for half in range(2)]
            intra = [[None] * n_pos for _ in range(2)]
            for i in range(n_pos):
                for j in range(i + 1):
                    prod = qd[i] * kd[j]
                    for half in range(2):
                        s = jnp.sum(jnp.where(halves[half], prod, 0.0), axis=-1, keepdims=True)
                        term = s * vp[j][:, half * GLA_DV:(half + 1) * GLA_DV]
                        intra[half][i] = term if intra[half][i] is None else intra[half][i] + term
            qs = stack(qd)
            lhs = stack([jnp.where(halves[half], qs, 0.0) for half in range(2)]).astype(BF16)
            vs = stack(vp).astype(BF16)
            btt = _dot_tn_exact_rhs01(bt, eye)
            inter = jnp.zeros((2 * R, GLA_DV), F32)
            for r in range(SUBLANES):
                S = sgla_in_ref[g * SUBLANES + r, p]
                inter = jnp.where(seq_of_row2 == r,
                                  jnp.dot(lhs, S.astype(BF16), preferred_element_type=F32), inter)
                upd = _dot_tn(jnp.where(seq_of_row == r, kt, 0.0).astype(BF16), vs)
                upd = jnp.where(krow < GLA_DK, upd[:, :GLA_DV], upd[:, GLA_DV:])
                dec = jnp.exp(jnp.broadcast_to(btt[:, r:r + 1], (PAIR_DK, GLA_DV)))
                sgla_ref[g * SUBLANES + r, p] = dec * S + upd
            for half in range(2):
                for i in range(n_pos):
                    co = i * O_W + RET_W + (2 * p + half) * GLA_DV
                    lo = half * R + i * SUBLANES
                    o_ref[rows, co:co + GLA_DV] = intra[half][i] + inter[lo:lo + SUBLANES]
        return carry

    lax.fori_loop(0, n_groups, body, 0)


def _head_act(o_ref, gate_ref, gn_ref, h):
    cs = slice(h * LANES, (h + 1) * LANES)
    oh = o_ref[:, cs]
    yh = oh * lax.rsqrt(jnp.mean(oh * oh, axis=-1, keepdims=True) + EPS) * gn_ref[:, cs]
    gh = gate_ref[:, cs]
    return (yh * (gh * _sigmoid(gh))).astype(BF16)


def _post_rows(a_r, a_g, gm_ref, x, wro_ref, wgo_ref, wo_ref, ln2_ref, wup_ref, cw_ref,
               cb_ref, wdn_ref, lnf_ref, conv_taps):
    blocks = [slice(lo, lo + MERGE_BLOCK) for lo in range(0, D_MODEL, MERGE_BLOCK)]

    def out_block(cs):
        return (jnp.dot(a_r, wro_ref[:, cs], preferred_element_type=F32),
                jnp.dot(a_g, wgo_ref[:, cs], preferred_element_type=F32))

    mix = []
    y_next = out_block(blocks[0])
    for i, cs in enumerate(blocks):
        y_r, y_g = y_next
        if i + 1 < len(blocks):
            y_next = out_block(blocks[i + 1])
        m_r = gm_ref[:, cs]
        m_g = gm_ref[:, D_MODEL + cs.start:D_MODEL + cs.stop]
        mix.append((_sigmoid(m_r) * y_r + _sigmoid(m_g) * y_g).astype(BF16))
    mix = jnp.concatenate(mix, axis=1)
    h1 = jnp.concatenate(
        [x[:, cs] + jnp.dot(mix, wo_ref[:, cs], preferred_element_type=F32) for cs in blocks],
        axis=1)
    n2, row_scale = _rms_split(h1, ln2_ref[...])

    nb = D_FF // FFN_BLOCK

    def up_block(j):
        lo = j * FFN_BLOCK
        return (jnp.dot(n2, wup_ref[:, lo:lo + FFN_BLOCK], preferred_element_type=F32),
                jnp.dot(n2, wup_ref[:, D_FF + lo:D_FF + lo + FFN_BLOCK],
                        preferred_element_type=F32))

    h2 = h1
    uv_next = up_block(0)
    for j in range(nb):
        cs = slice(j * FFN_BLOCK, (j + 1) * FFN_BLOCK)
        u, vv = (row_scale(t) for t in uv_next)
        if j + 1 < nb:
            uv_next = up_block(j + 1)
        u_m2, u_m1 = conv_taps(cs, u)
        uc = cb_ref[:, cs] + u_m2 * cw_ref[0:1, cs]
        uc = uc + u_m1 * cw_ref[1:2, cs]
        uc = uc + u * cw_ref[2:3, cs]
        act = 0.5 * uc * (1.0 + lax.erf(uc * float(np.float32(np.sqrt(0.5))))) * vv
        h2 = h2 + _dot(act, wdn_ref[cs, :])
    return _rms(h2, lnf_ref[...])


def _post_prompt_rows(act_ref, gm_ref, x_ref, w_refs, y_ref, tail_ref, ubuf_ref):
    T = x_ref.shape[0]

    def conv_taps(cs, u):
        ubuf_ref[SUBLANES:SUBLANES + T, cs] = u
        return (ubuf_ref[SUBLANES - 2:SUBLANES - 2 + T, cs],
                ubuf_ref[SUBLANES - 1:SUBLANES - 1 + T, cs])

    y_ref[...] = _post_rows(act_ref[:, :RET_W], act_ref[:, RET_W:], gm_ref, x_ref[...], *w_refs,
                            conv_taps)
    tail_ref[...] = ubuf_ref[T + SUBLANES - (CONV_WIDTH - 1):T + SUBLANES, :]
    ubuf_ref[0:SUBLANES, :] = ubuf_ref[T:T + SUBLANES, :]


def _post_sample_kernel(o_ref, gate_ref, x_ref, cache_ref, *rest):
    w_refs, (y_ref, tail_ref, ubuf_ref) = rest[:10], rest[10:]
    n_seq, n_pos, _ = x_ref.shape
    for k in range(CONV_WIDTH - 1):
        ubuf_ref[k] = cache_ref[:, k, :]

    class PosStacked:
        def __init__(self, ref, width, base=0):
            self.ref, self.width, self.base = ref, width, base

        def __getitem__(self, idx):
            cs = idx[1]
            return jnp.concatenate(
                [self.ref[:, j * self.width + self.base + cs.start:
                          j * self.width + self.base + cs.stop] for j in range(n_pos)], axis=0)

    def conv_taps(cs, u):
        c0, c1 = ubuf_ref[0, :, cs], ubuf_ref[1, :, cs]
        ubuf_ref[0, :, cs] = u[(n_pos - 2) * n_seq:(n_pos - 1) * n_seq]
        ubuf_ref[1, :, cs] = u[(n_pos - 1) * n_seq:]
        return (jnp.concatenate([c0, c1, u[:(n_pos - 2) * n_seq]], axis=0),
                jnp.concatenate([c1, u[:(n_pos - 1) * n_seq]], axis=0))

    gn_ref, w_refs = w_refs[0], w_refs[1:]
    o_all = PosStacked(o_ref, O_W)
    gate_all = PosStacked(gate_ref, GATE_W)
    acts = [_head_act(o_all, gate_all, gn_ref, h) for h in range(RET_HEADS + GLA_HEADS)]
    x_all = jnp.concatenate([x_ref[:, j, :] for j in range(n_pos)], axis=0)
    y = _post_rows(jnp.concatenate(acts[:RET_HEADS], axis=1),
                   jnp.concatenate(acts[RET_HEADS:], axis=1),
                   PosStacked(gate_ref, GATE_W, O_W), x_all, *w_refs, conv_taps)
    for j in range(n_pos):
        y_ref[:, j, :] = y[j * n_seq:(j + 1) * n_seq]
    for k in range(CONV_WIDTH - 1):
        tail_ref[:, k, :] = ubuf_ref[k]


def _const_spec(shape):
    nd = len(shape)
    return pl.BlockSpec(shape, lambda *_: (0,) * nd, pipeline_mode=pl.Buffered(1))


def _params(n_grid):
    return pltpu.CompilerParams(dimension_semantics=("arbitrary",) * n_grid,
                                vmem_limit_bytes=VMEM_LIMIT)


def _rope_tables(pos):
    half = RET_DK // 2
    inv = ROPE_BASE ** (-jnp.arange(half, dtype=F32) / half)
    ang = pos[:, None] * inv[None, :]
    cos, sin = jnp.cos(ang), jnp.sin(ang)
    return jnp.concatenate([cos, cos], axis=-1), jnp.concatenate([-sin, sin], axis=-1)


def kernel(x_prompt, x_sample, state_ret, state_gla, cache_conv, ln1, w_in, w_gate_up, b_gate_up,
           g_ret, g_gla, w_ret_out, w_gla_out, w_o, ln2, w_up, conv_w, conv_b, w_down, ln_f):
    Bp, Lp, _ = x_prompt.shape
    Bs, Ls, _ = x_sample.shape
    assert state_ret.shape[0] == 1, "single layer"
    assert Ls >= CONV_WIDTH - 1 and Bs % SUBLANES == 0

    c = IN_COLS
    proj_w = [ln1[0][None, :],
              w_in[0][:, :MAIN_W].astype(BF16),
              w_in[0][:, c[9]:].astype(BF16),
              jnp.pad(w_in[0][:, c[8]:c[9]], ((0, 0), (0, LANES - GLA_GATE_RANK))).astype(BF16),
              jnp.pad(w_gate_up[0], ((0, LANES - GLA_GATE_RANK), (0, 0))).astype(BF16),
              b_gate_up[0][None, :]]
    post_w = [jnp.concatenate([g_ret[0], g_gla[0]])[None, :], w_ret_out[0].astype(BF16),
              w_gla_out[0].astype(BF16), w_o[0].astype(BF16), ln2[0][None, :],
              w_up[0].astype(BF16), conv_w[0], conv_b[0][None, :], w_down[0].astype(BF16),
              ln_f[None, :]]
    proj_w_specs = [_const_spec(w.shape) for w in proj_w]
    post_w_specs = [_const_spec(w.shape) for w in post_w]
    cos_p, sin_p = _rope_tables(jnp.arange(Lp, dtype=F32))
    cos_s, sin_s = _rope_tables(PAST_LEN + jnp.arange(Ls, dtype=F32))
    sds = lambda *shape: jax.ShapeDtypeStruct(shape, F32)
    full = lambda *shape: pl.BlockSpec(shape, lambda *_: (0,) * len(shape))

    T = 256
    grid_p = (Bp, Lp // T)
    rows_p = lambda w: pl.BlockSpec((None, T, w), lambda b, t: (b, t, 0))
    state_p = lambda *shape: pl.BlockSpec((None,) + shape, lambda b, t: (b,) + (0,) * len(shape))
    rope_spec = pl.BlockSpec((T, RET_DK), lambda b, t: (t, 0))
    y_p, tail_p, sret_p, sgla_p = pl.pallas_call(
        _layer_prompt_kernel, grid=grid_p,
        in_specs=[rows_p(D_MODEL), rope_spec, rope_spec] + proj_w_specs + post_w_specs,
        out_specs=[rows_p(D_MODEL), state_p(CONV_WIDTH - 1, D_FF),
                   state_p(RET_HEADS, RET_DK, RET_DV), state_p(GLA_PAIRS, PAIR_DK, GLA_DV)],
        out_shape=[sds(Bp, Lp, D_MODEL), sds(Bp, CONV_WIDTH - 1, D_FF),
                   sds(Bp, RET_HEADS, RET_DK, RET_DV), sds(Bp, GLA_PAIRS, PAIR_DK, GLA_DV)],
        scratch_shapes=[
            pltpu.VMEM((T, SCAN_W), F32), pltpu.VMEM((T, GLA_QK_W), F32),
            pltpu.VMEM((T, O_W), F32), pltpu.VMEM((T, O_W), F32),
            pltpu.VMEM((T, 2 * D_MODEL), F32), pltpu.VMEM((T, O_W), BF16),
            pltpu.VMEM((T // RET_CHUNK, RET_HEADS, RET_DK + RET_CHUNK, RET_DV), BF16),
            pltpu.VMEM((T // CHUNK, GLA_PAIRS, PAIR_DK + 2 * CHUNK, GLA_DV), BF16),
            pltpu.VMEM((T + SUBLANES, D_FF), F32)],
        compiler_params=_params(2), name="layer_prompt",
    )(x_prompt, cos_p, sin_p, *proj_w, *post_w)

    scan_s, gate_s, la_s = pl.pallas_call(
        _proj_sample_kernel, grid=(1,),
        in_specs=[_const_spec((Bs, Ls, D_MODEL)), full(Ls, RET_DK), full(Ls, RET_DK)]
        + proj_w_specs,
        out_specs=[full(Bs, Ls * SCAN_W), full(Bs, Ls * GATE_W), full(Bs, Ls * GLA_QK_W)],
        out_shape=[sds(Bs, Ls * SCAN_W), sds(Bs, Ls * GATE_W), sds(Bs, Ls * GLA_QK_W)],
        compiler_params=_params(1), name="proj_sample",
    )(x_sample, cos_s, sin_s, *proj_w)

    SB = 16
    rows_s = lambda w: pl.BlockSpec((SB, w), lambda i: (i, 0))
    state_s = lambda *shape: pl.BlockSpec((SB,) + shape, lambda i: (i,) + (0,) * len(shape))
    o_s, sret_s, sgla_s = pl.pallas_call(
        functools.partial(_scan_sample_kernel, Ls), grid=(Bs // SB,),
        in_specs=[rows_s(Ls * SCAN_W), rows_s(Ls * GLA_QK_W),
                  state_s(RET_HEADS, RET_DK, RET_DV), state_s(GLA_PAIRS, PAIR_DK, GLA_DV)],
        out_specs=[rows_s(Ls * O_W), state_s(RET_HEADS, RET_DK, RET_DV),
                   state_s(GLA_PAIRS, PAIR_DK, GLA_DV)],
        out_shape=[sds(Bs, Ls * O_W), sds(Bs, RET_HEADS, RET_DK, RET_DV),
                   sds(Bs, GLA_PAIRS, PAIR_DK, GLA_DV)],
        compiler_params=_params(1), name="scan_sample",
    )(scan_s, la_s, state_ret[0], state_gla[0].reshape(Bs, GLA_PAIRS, PAIR_DK, GLA_DV))

    y_s, tail_s = pl.pallas_call(
        _post_sample_kernel, grid=(1,),
        in_specs=[_const_spec(s) for s in ((Bs, Ls * O_W), (Bs, Ls * GATE_W), (Bs, Ls, D_MODEL),
                                           (Bs, CONV_WIDTH - 1, D_FF))] + post_w_specs,
        out_specs=[full(Bs, Ls, D_MODEL), full(Bs, CONV_WIDTH - 1, D_FF)],
        out_shape=[sds(Bs, Ls, D_MODEL), sds(Bs, CONV_WIDTH - 1, D_FF)],
        scratch_shapes=[pltpu.VMEM((CONV_WIDTH - 1, Bs, D_FF), F32)],
        compiler_params=_params(1), name="post_sample",
    )(o_s, gate_s, x_sample, cache_conv[0], *post_w)

    gshape = (1, -1, GLA_HEADS, GLA_DK, GLA_DV)
    return (y_p, y_s, sret_p[None], sret_s[None], sgla_p.reshape(gshape), sgla_s.reshape(gshape),
            tail_p[None], tail_s[None])
```

```python
import functools
import math

import jax
import jax.numpy as jnp
import numpy as np
from jax import lax
from jax.experimental import pallas as pl
from jax.experimental.pallas import tpu as pltpu

D_MODEL = 1024
PAST_LEN = 16384
RET_HEADS = 4
RET_DK = 128
RET_DV = 128
GLA_HEADS = 4
GLA_DK = 64
GLA_DV = 128
GLA_GATE_RANK = 16
GLA_GATE_TAU = 16.0
D_FF = 2816
CONV_WIDTH = 3
CHUNK = 64
RET_CHUNK = 128
ROPE_BASE = 10000.0
EPS = 1e-6

LANES = 128
SUBLANES = 8
VMEM_LIMIT = 56 * 1024 * 1024
RET_W = RET_HEADS * RET_DK
GLA_QK_W = GLA_HEADS * GLA_DK
GLA_V_W = GLA_HEADS * GLA_DV
GLA_PAIRS = GLA_HEADS // 2
PAIR_DK = 2 * GLA_DK
SCAN_W = 3 * RET_W + 2 * GLA_QK_W + GLA_V_W
GATE_W = RET_W + GLA_V_W + 2 * D_MODEL
O_W = RET_W + GLA_V_W
FFN_BLOCK = 256
MERGE_BLOCK = 256
GATE_BLOCK = 256
LOG_GAMMA = tuple(math.log(1.0 - 2.0 ** (-5.0 - h)) for h in range(RET_HEADS))
IN_COLS = tuple(np.cumsum((0, RET_W, RET_W, RET_W, RET_W, GLA_QK_W, GLA_QK_W, GLA_V_W, GLA_V_W,
                           GLA_GATE_RANK, D_MODEL, D_MODEL)).tolist())
MAIN_W = IN_COLS[8]

BF16 = jnp.bfloat16
F32 = jnp.float32


def _dot(a, b):
    return jnp.dot(a.astype(BF16), b.astype(BF16), preferred_element_type=F32)


def _dot_nt(a, b):
    return lax.dot_general(a, b, (((1,), (1,)), ((), ())), preferred_element_type=F32)


def _dot_tn(a, b):
    return lax.dot_general(a, b, (((0,), (0,)), ((), ())), preferred_element_type=F32)


def _split3(x):
    hi = x.astype(BF16)
    r = x - hi.astype(F32)
    mid = r.astype(BF16)
    lo = (r - mid.astype(F32)).astype(BF16)
    return hi, mid, lo


def _dot_exact_lhs01(m01, x):
    m = m01.astype(BF16)
    return sum(jnp.dot(m, t, preferred_element_type=F32) for t in _split3(x))


def _dot_tn_exact_rhs01(x, m01):
    m = m01.astype(BF16)
    return sum(_dot_tn(t, m) for t in _split3(x))


def _rms(x, g):
    return x * lax.rsqrt(jnp.mean(x * x, axis=-1, keepdims=True) + EPS) * g


def _rms_split(x, g):
    r = lax.rsqrt(jnp.mean(x * x, axis=-1, keepdims=True) + EPS)
    rb = jnp.broadcast_to(r, (x.shape[0], LANES))
    scale = lambda v: v * jnp.concatenate([rb] * (v.shape[1] // LANES), axis=1)
    return (x * g).astype(BF16), scale


def _sigmoid(x):
    return 0.5 * jnp.tanh(0.5 * x) + 0.5


def _proj_rows(x, cos, sin, ln1_ref, wmain_ref, wm_ref, wga_ref, wgu_ref, bgu_ref,
               put_scan, put_gate, put_la):
    n, row_scale = _rms_split(x, ln1_ref[...])
    mm = lambda ref, lo, hi: row_scale(jnp.dot(n, ref[:, lo:hi], preferred_element_type=F32))
    c = IN_COLS
    qk = mm(wmain_ref, c[0], c[2])
    ga = mm(wga_ref, 0, LANES)
    z = _dot(ga, wgu_ref[...]) + bgu_ref[...]
    log_sig = -(jnp.maximum(-z, 0.0) + jnp.log1p(jnp.exp(-jnp.abs(z))))
    put_la(0, log_sig / GLA_GATE_TAU)
    put_scan(2 * RET_W, mm(wmain_ref, c[2], c[3]))
    gqk = mm(wmain_ref, c[4], c[6])
    put_scan(3 * RET_W, gqk[:, :GLA_QK_W] * (GLA_DK ** -0.5))
    put_scan(3 * RET_W + GLA_QK_W, gqk[:, GLA_QK_W:])
    put_scan(3 * RET_W + 2 * GLA_QK_W, mm(wmain_ref, c[6], c[7]))
    for g in range(2 * RET_HEADS):
        lo = g * RET_DK
        t = qk[:, lo:lo + RET_DK]
        r = t * cos + pltpu.roll(t, RET_DK // 2, axis=1) * sin
        if g >= RET_HEADS:
            r = r * (RET_DK ** -0.5)
        put_scan(lo, r)
    jobs = []
    for ref, src, dst, width in ((wmain_ref, c[3], 0, RET_W), (wmain_ref, c[7], RET_W, GLA_V_W),
                                 (wm_ref, 0, O_W, 2 * D_MODEL)):
        for b in range(0, width, GATE_BLOCK):
            jobs.append(functools.partial(
                lambda ref, lo, dst: put_gate(dst, mm(ref, lo, lo + GATE_BLOCK)),
                ref, src + b, dst + b))
    return jobs


def _putter(ref, base=0):
    def put(lo, val):
        ref[:, base + lo:base + lo + val.shape[1]] = val
    return put


def _proj_sample_kernel(x_ref, cos_ref, sin_ref, *rest):
    w_refs, (scan_ref, gate_ref, la_ref) = rest[:6], rest[6:]
    n_seq, n_pos, _ = x_ref.shape
    by_pos = lambda f: jnp.concatenate([f(j) for j in range(n_pos)], axis=0)
    table = lambda ref: by_pos(lambda j: jnp.broadcast_to(ref[j:j + 1, :], (n_seq, RET_DK)))

    def pos_putter(ref, width):
        def put(lo, val):
            for j in range(n_pos):
                ref[:, j * width + lo:j * width + lo + val.shape[1]] = (
                    val[j * n_seq:(j + 1) * n_seq])
        return put

    for job in _proj_rows(by_pos(lambda j: x_ref[:, j, :]), table(cos_ref), table(sin_ref), *w_refs,
                          pos_putter(scan_ref, SCAN_W), pos_putter(gate_ref, GATE_W),
                          pos_putter(la_ref, GLA_QK_W)):
        job()


N_PROJ_W = 6
N_POST_W = 9


def _layer_prompt_kernel(x_ref, cos_ref, sin_ref, *rest):
    proj_w, gn_ref = rest[:N_PROJ_W], rest[N_PROJ_W]
    post_w = rest[N_PROJ_W + 1:N_PROJ_W + 1 + N_POST_W]
    (y_ref, tail_ref, sret_ref, sgla_ref, qkv_ref, la_ref, o_ref, og_ref, gm_ref, act_ref,
     rhs_ret_ref, rhs_gla_ref, ubuf_ref) = rest[N_PROJ_W + 1 + N_POST_W:]

    @pl.when(pl.program_id(1) == 0)
    def _():
        sret_ref[...] = jnp.zeros_like(sret_ref)
        sgla_ref[...] = jnp.zeros_like(sgla_ref)
        ubuf_ref[0:SUBLANES, :] = jnp.zeros((SUBLANES, D_FF), F32)

    _attn_prompt_rows(x_ref, cos_ref, sin_ref, proj_w, gn_ref, act_ref, gm_ref, sret_ref, sgla_ref,
                      qkv_ref, la_ref, o_ref, og_ref, rhs_ret_ref, rhs_gla_ref)
    _post_prompt_rows(act_ref, gm_ref, x_ref, post_w, y_ref, tail_ref, ubuf_ref)


def _attn_prompt_rows(x_ref, cos_ref, sin_ref, w_refs, gn_ref, act_ref, gm_ref, sret_ref, sgla_ref,
                      qkv_ref, la_ref, o_ref, og_ref, rhs_ret_ref, rhs_gla_ref):
    def put_gate(lo, val):
        if lo < O_W:
            og_ref[:, lo:lo + val.shape[1]] = val
        else:
            gm_ref[:, lo - O_W:lo - O_W + val.shape[1]] = val

    jobs = _proj_rows(x_ref[...], cos_ref[...], sin_ref[...], *w_refs,
                      _putter(qkv_ref), put_gate, _putter(la_ref))

    T = x_ref.shape[0]
    n_heads = RET_HEADS + GLA_HEADS
    n_ticks = RET_HEADS * (T // RET_CHUNK) + GLA_PAIRS * (T // CHUNK) + 2
    n_scan_jobs = len(jobs) - (n_heads - 2)
    ticks = [0]

    def tick():
        ticks[0] += 1
        due = min(ticks[0] * n_scan_jobs // n_ticks, n_scan_jobs)
        while n_scan_jobs - (len(jobs) - (n_heads - 2)) < due:
            jobs.pop(0)()

    _scan_prompt_rows(qkv_ref, la_ref, o_ref, sret_ref, sgla_ref, rhs_ret_ref, rhs_gla_ref, tick)
    for h in range(n_heads):
        if jobs:
            jobs.pop(0)()
        act_ref[:, h * LANES:(h + 1) * LANES] = _head_act(o_ref, og_ref, gn_ref, h)
    assert not jobs


def _scan_prompt_rows(qkv_ref, la_ref, o_ref, sret_ref, sgla_ref, rhs_ret_ref, rhs_gla_ref, tick):
    T = qkv_ref.shape[0]
    n_rc = T // RET_CHUNK
    n_gc = T // CHUNK

    ri = lax.broadcasted_iota(jnp.int32, (RET_CHUNK, RET_CHUNK), 0)
    ci = lax.broadcasted_iota(jnp.int32, (RET_CHUNK, RET_CHUNK), 1)
    diff = (ri - ci).astype(F32)
    pos = (lax.broadcasted_iota(jnp.int32, (T, RET_DK), 0) % RET_CHUNK).astype(F32)
    ret_lhs, ret_upd = [], []
    for h in range(RET_HEADS):
        lg = LOG_GAMMA[h]
        q = qkv_ref[:, h * RET_DK:(h + 1) * RET_DK]
        k = qkv_ref[:, RET_W + h * RET_DK:RET_W + (h + 1) * RET_DK]
        vb = qkv_ref[:, 2 * RET_W + h * RET_DV:2 * RET_W + (h + 1) * RET_DV].astype(BF16)
        qb = q.astype(BF16)
        kb = k.astype(BF16)
        qd = (q * jnp.exp(lg * (pos + 1.0))).astype(BF16)
        kt = (k * jnp.exp(lg * (RET_CHUNK - 1.0 - pos))).astype(BF16)
        dm = jnp.where(ri >= ci, jnp.exp(lg * diff), 0.0)
        for c in range(n_rc):
            rc = slice(c * RET_CHUNK, (c + 1) * RET_CHUNK)
            rhs_ret_ref[c, h, RET_DK:, :] = vb[rc]
            ret_upd.append(_dot_tn(kt[rc], vb[rc]))
            s = _dot_nt(qb[rc], kb[rc]) * dm
            ret_lhs.append(jnp.concatenate([qd[rc], s.astype(BF16)], axis=1))
            tick()

    ti = lax.broadcasted_iota(jnp.int32, (T, T), 0)
    tj = lax.broadcasted_iota(jnp.int32, (T, T), 1)
    in_chunk_causal = ((ti // CHUNK) == (tj // CHUNK)) & (ti >= tj)
    go = 3 * RET_W
    bc = _dot_exact_lhs01(in_chunk_causal, la_ref[...])
    tick()
    bt = jnp.concatenate(
        [jnp.broadcast_to(bc[(c + 1) * CHUNK - 1:(c + 1) * CHUNK, :], (CHUNK, GLA_QK_W))
         for c in range(n_gc)], axis=0)
    gk = qkv_ref[:, go + GLA_QK_W:go + 2 * GLA_QK_W]
    qd_all = qkv_ref[:, go:go + GLA_QK_W] * jnp.exp(bc)
    kd_all = (gk * jnp.exp(-bc)).astype(BF16)
    kt_all = (gk * jnp.exp(bt - bc)).astype(BF16)
    sel = (lax.broadcasted_iota(jnp.int32, (T, LANES), 0)
           == lax.broadcasted_iota(jnp.int32, (T, LANES), 1) * CHUNK + (CHUNK - 1))
    bl = _dot_tn_exact_rhs01(bc, sel)
    tick()
    lane = lax.broadcasted_iota(jnp.int32, (1, PAIR_DK), 1)
    krow = lax.broadcasted_iota(jnp.int32, (PAIR_DK, 1), 0)
    si = lax.broadcasted_iota(jnp.int32, (CHUNK, 2 * CHUNK), 0)
    sj = lax.broadcasted_iota(jnp.int32, (CHUNK, 2 * CHUNK), 1)
    half_causal = [(sj // CHUNK == half) & (si >= sj % CHUNK) for half in range(2)]
    gla_lhs, gla_upd, gla_dec = [], [], []
    for p in range(GLA_PAIRS):
        cs = slice(p * PAIR_DK, (p + 1) * PAIR_DK)
        vo = go + 2 * GLA_QK_W + p * 2 * GLA_DV
        vp = qkv_ref[:, vo:vo + 2 * GLA_DV].astype(BF16)
        for c in range(n_gc):
            rc = slice(c * CHUNK, (c + 1) * CHUNK)
            rhs_gla_ref[c, p, PAIR_DK:PAIR_DK + CHUNK, :] = vp[rc, :GLA_DV]
            rhs_gla_ref[c, p, PAIR_DK + CHUNK:, :] = vp[rc, GLA_DV:]
            upd = _dot_tn(kt_all[rc, cs], vp[rc])
            gla_upd.append(jnp.where(krow < GLA_DK, upd[:, :GLA_DV], upd[:, GLA_DV:]))
            gla_dec.append(jnp.exp(jnp.broadcast_to(bl[cs, c:c + 1], (PAIR_DK, GLA_DV))))
            kk = jnp.concatenate([kd_all[rc, cs]] * 2, axis=0)
            blocks = []
            for half in range(2):
                qm = jnp.where((lane // GLA_DK) == half, qd_all[rc, cs], 0.0).astype(BF16)
                s = jnp.where(half_causal[half], _dot_nt(qm, kk), 0.0)
                blocks.append(jnp.concatenate([qm, s.astype(BF16)], axis=1))
            gla_lhs.append(jnp.concatenate(blocks, axis=0))
            tick()

    for h in range(RET_HEADS):
        S = sret_ref[h]
        for c in range(n_rc):
            rhs_ret_ref[c, h, :RET_DK, :] = S.astype(BF16)
            S = math.exp(LOG_GAMMA[h] * RET_CHUNK) * S + ret_upd[h * n_rc + c]
        sret_ref[h] = S
    for p in range(GLA_PAIRS):
        S = sgla_ref[p]
        for c in range(n_gc):
            rhs_gla_ref[c, p, :PAIR_DK, :] = S.astype(BF16)
            S = gla_dec[p * n_gc + c] * S + gla_upd[p * n_gc + c]
        sgla_ref[p] = S
    for h in range(RET_HEADS):
        for c in range(n_rc):
            o_ref[c * RET_CHUNK:(c + 1) * RET_CHUNK, h * RET_DV:(h + 1) * RET_DV] = jnp.dot(
                ret_lhs[h * n_rc + c], rhs_ret_ref[c, h], preferred_element_type=F32)
    for p in range(GLA_PAIRS):
        for c in range(n_gc):
            o2 = jnp.dot(gla_lhs[p * n_gc + c], rhs_gla_ref[c, p], preferred_element_type=F32)
            for half in range(2):
                co = RET_W + (2 * p + half) * GLA_DV
                o_ref[c * CHUNK:(c + 1) * CHUNK, co:co + GLA_DV] = o2[half * CHUNK:(half + 1) * CHUNK]


def _scan_sample_kernel(n_pos, qkv_ref, la_ref, sret_in_ref, sgla_in_ref, o_ref, sret_ref, sgla_ref):
    n_groups = qkv_ref.shape[0] // SUBLANES
    R = n_pos * SUBLANES
    seq_of_row = lax.broadcasted_iota(jnp.int32, (R, 1), 0) % SUBLANES
    seq_of_row2 = lax.broadcasted_iota(jnp.int32, (2 * R, 1), 0) % SUBLANES
    lane = lax.broadcasted_iota(jnp.int32, (1, PAIR_DK), 1)
    krow = lax.broadcasted_iota(jnp.int32, (PAIR_DK, 1), 0)
    eye = (lax.broadcasted_iota(jnp.int32, (SUBLANES, LANES), 0)
           == lax.broadcasted_iota(jnp.int32, (SUBLANES, LANES), 1))
    stack = lambda ts: jnp.concatenate(ts, axis=0)

    def body(g, carry):
        rows = pl.ds(pl.multiple_of(g * SUBLANES, SUBLANES), SUBLANES)
        col = lambda j, lo, w: qkv_ref[rows, j * SCAN_W + lo:j * SCAN_W + lo + w]

        for h in range(RET_HEADS):
            lg = LOG_GAMMA[h]
            q = [col(j, h * RET_DK, RET_DK) for j in range(n_pos)]
            k = [col(j, RET_W + h * RET_DK, RET_DK) for j in range(n_pos)]
            v = [col(j, 2 * RET_W + h * RET_DV, RET_DV) for j in range(n_pos)]
            intra = []
            for i in range(n_pos):
                acc = None
                for j in range(i + 1):
                    s = jnp.sum(q[i] * k[j], axis=-1, keepdims=True) * math.exp(lg * (i - j))
                    acc = s * v[j] if acc is None else acc + s * v[j]
                intra.append(acc)
            qd = stack([q[i] * math.exp(lg * (i + 1)) for i in range(n_pos)]).astype(BF16)
            kt = stack([k[j] * math.exp(lg * (n_pos - 1 - j)) for j in range(n_pos)])
            vs = stack(v).astype(BF16)
            inter = jnp.zeros((R, RET_DV), F32)
            for r in range(SUBLANES):
                mine = seq_of_row == r
                S = sret_in_ref[g * SUBLANES + r, h]
                inter = jnp.where(mine, jnp.dot(qd, S.astype(BF16), preferred_element_type=F32),
                                  inter)
                upd = _dot_tn(jnp.where(mine, kt, 0.0).astype(BF16), vs)
                sret_ref[g * SUBLANES + r, h] = math.exp(lg * n_pos) * S + upd
            for i in range(n_pos):
                o_ref[rows, i * O_W + h * RET_DV:i * O_W + (h + 1) * RET_DV] = (
                    intra[i] + inter[i * SUBLANES:(i + 1) * SUBLANES])

        go = 3 * RET_W
        for p in range(GLA_PAIRS):
            la = [la_ref[rows, j * GLA_QK_W + p * PAIR_DK:j * GLA_QK_W + (p + 1) * PAIR_DK]
                  for j in range(n_pos)]
            bc = [la[0]]
            for j in range(1, n_pos):
                bc.append(bc[-1] + la[j])
            bt = bc[-1]
            gq = [col(j, go + p * PAIR_DK, PAIR_DK) for j in range(n_pos)]
            gk = [col(j, go + GLA_QK_W + p * PAIR_DK, PAIR_DK) for j in range(n_pos)]
            vp = [col(j, go + 2 * GLA_QK_W + p * 2 * GLA_DV, 2 * GLA_DV) for j in range(n_pos)]
            qd = [gq[j] * jnp.exp(bc[j]) for j in range(n_pos)]
            kd = [gk[j] * jnp.exp(-bc[j]) for j in range(n_pos)]
            kt = stack([gk[j] * jnp.exp(bt - bc[j]) for j in range(n_pos)])
            halves = [(lane // GLA_DK) == half for half in range(2)]
            intra = [[None] * n_pos for _ in range(2)]
            for i in range(n_pos):
                for j in range(i + 1):
                    prod = qd[i] * kd[j]
                    for half in range(2):
                        s = jnp.sum(jnp.where(halves[half], prod, 0.0), axis=-1, keepdims=True)
                        term = s * vp[j][:, half * GLA_DV:(half + 1) * GLA_DV]
                        intra[half][i] = term if intra[half][i] is None else intra[half][i] + term
            qs = stack(qd)
            lhs = stack([jnp.where(halves[half], qs, 0.0) for half in range(2)]).astype(BF16)
            vs = stack(vp).astype(BF16)
            btt = _dot_tn_exact_rhs01(bt, eye)
            inter = jnp.zeros((2 * R, GLA_DV), F32)
            for r in range(SUBLANES):
                S = sgla_in_ref[g * SUBLANES + r, p]
                inter = jnp.where(seq_of_row2 == r,
                                  jnp.dot(lhs, S.astype(BF16), preferred_element_type=F32), inter)
                upd = _dot_tn(jnp.where(seq_of_row == r, kt, 0.0).astype(BF16), vs)
                upd = jnp.where(krow < GLA_DK, upd[:, :GLA_DV], upd[:, GLA_DV:])
                dec = jnp.exp(jnp.broadcast_to(btt[:, r:r + 1], (PAIR_DK, GLA_DV)))
                sgla_ref[g * SUBLANES + r, p] = dec * S + upd
            for half in range(2):
                for i in range(n_pos):
                    co = i * O_W + RET_W + (2 * p + half) * GLA_DV
                    lo = half * R + i * SUBLANES
                    o_ref[rows, co:co + GLA_DV] = intra[half][i] + inter[lo:lo + SUBLANES]
        return carry

    lax.fori_loop(0, n_groups, body, 0)


def _head_act(o_ref, gate_ref, gn_ref, h):
    cs = slice(h * LANES, (h + 1) * LANES)
    oh = o_ref[:, cs]
    yh = oh * lax.rsqrt(jnp.mean(oh * oh, axis=-1, keepdims=True) + EPS) * gn_ref[:, cs]
    gh = gate_ref[:, cs]
    return (yh * (gh * _sigmoid(gh))).astype(BF16)


def _post_rows(a_r, a_g, gm_ref, x, wro_ref, wgo_ref, wo_ref, ln2_ref, wup_ref, cw_ref,
               cb_ref, wdn_ref, lnf_ref, conv_taps):
    blocks = [slice(lo, lo + MERGE_BLOCK) for lo in range(0, D_MODEL, MERGE_BLOCK)]

    def out_block(cs):
        return (jnp.dot(a_r, wro_ref[:, cs], preferred_element_type=F32),
                jnp.dot(a_g, wgo_ref[:, cs], preferred_element_type=F32))

    mix = []
    y_next = out_block(blocks[0])
    for i, cs in enumerate(blocks):
        y_r, y_g = y_next
        if i + 1 < len(blocks):
            y_next = out_block(blocks[i + 1])
        m_r = gm_ref[:, cs]
        m_g = gm_ref[:, D_MODEL + cs.start:D_MODEL + cs.stop]
        mix.append((_sigmoid(m_r) * y_r + _sigmoid(m_g) * y_g).astype(BF16))
    mix = jnp.concatenate(mix, axis=1)
    h1 = jnp.concatenate(
        [x[:, cs] + jnp.dot(mix, wo_ref[:, cs], preferred_element_type=F32) for cs in blocks],
        axis=1)
    n2, row_scale = _rms_split(h1, ln2_ref[...])

    nb = D_FF // FFN_BLOCK

    def up_block(j):
        lo = j * FFN_BLOCK
        return (jnp.dot(n2, wup_ref[:, lo:lo + FFN_BLOCK], preferred_element_type=F32),
                jnp.dot(n2, wup_ref[:, D_FF + lo:D_FF + lo + FFN_BLOCK],
                        preferred_element_type=F32))

    h2 = h1
    uv_next = up_block(0)
    for j in range(nb):
        cs = slice(j * FFN_BLOCK, (j + 1) * FFN_BLOCK)
        u, vv = (row_scale(t) for t in uv_next)
        if j + 1 < nb:
            uv_next = up_block(j + 1)
        u_m2, u_m1 = conv_taps(cs, u)
        uc = cb_ref[:, cs] + u_m2 * cw_ref[0:1, cs]
        uc = uc + u_m1 * cw_ref[1:2, cs]
        uc = uc + u * cw_ref[2:3, cs]
        act = 0.5 * uc * (1.0 + lax.erf(uc * float(np.float32(np.sqrt(0.5))))) * vv
        h2 = h2 + _dot(act, wdn_ref[cs, :])
    return _rms(h2, lnf_ref[...])


def _post_prompt_rows(act_ref, gm_ref, x_ref, w_refs, y_ref, tail_ref, ubuf_ref):
    T = x_ref.shape[0]

    def conv_taps(cs, u):
        ubuf_ref[SUBLANES:SUBLANES + T, cs] = u
        return (ubuf_ref[SUBLANES - 2:SUBLANES - 2 + T, cs],
                ubuf_ref[SUBLANES - 1:SUBLANES - 1 + T, cs])

    y_ref[...] = _post_rows(act_ref[:, :RET_W], act_ref[:, RET_W:], gm_ref, x_ref[...], *w_refs,
                            conv_taps)
    tail_ref[...] = ubuf_ref[T + SUBLANES - (CONV_WIDTH - 1):T + SUBLANES, :]
    ubuf_ref[0:SUBLANES, :] = ubuf_ref[T:T + SUBLANES, :]


def _post_sample_kernel(o_ref, gate_ref, x_ref, cache_ref, *rest):
    w_refs, (y_ref, tail_ref, ubuf_ref) = rest[:10], rest[10:]
    n_seq, n_pos, _ = x_ref.shape
    for k in range(CONV_WIDTH - 1):
        ubuf_ref[k] = cache_ref[:, k, :]

    class PosStacked:
        def __init__(self, ref, width, base=0):
            self.ref, self.width, self.base = ref, width, base

        def __getitem__(self, idx):
            cs = idx[1]
            return jnp.concatenate(
                [self.ref[:, j * self.width + self.base + cs.start:
                          j * self.width + self.base + cs.stop] for j in range(n_pos)], axis=0)

    def conv_taps(cs, u):
        c0, c1 = ubuf_ref[0, :, cs], ubuf_ref[1, :, cs]
        ubuf_ref[0, :, cs] = u[(n_pos - 2) * n_seq:(n_pos - 1) * n_seq]
        ubuf_ref[1, :, cs] = u[(n_pos - 1) * n_seq:]
        return (jnp.concatenate([c0, c1, u[:(n_pos - 2) * n_seq]], axis=0),
                jnp.concatenate([c1, u[:(n_pos - 1) * n_seq]], axis=0))

    gn_ref, w_refs = w_refs[0], w_refs[1:]
    o_all = PosStacked(o_ref, O_W)
    gate_all = PosStacked(gate_ref, GATE_W)
    acts = [_head_act(o_all, gate_all, gn_ref, h) for h in range(RET_HEADS + GLA_HEADS)]
    x_all = jnp.concatenate([x_ref[:, j, :] for j in range(n_pos)], axis=0)
    y = _post_rows(jnp.concatenate(acts[:RET_HEADS], axis=1),
                   jnp.concatenate(acts[RET_HEADS:], axis=1),
                   PosStacked(gate_ref, GATE_W, O_W), x_all, *w_refs, conv_taps)
    for j in range(n_pos):
        y_ref[:, j, :] = y[j * n_seq:(j + 1) * n_seq]
    for k in range(CONV_WIDTH - 1):
        tail_ref[:, k, :] = ubuf_ref[k]


def _const_spec(shape):
    nd = len(shape)
    return pl.BlockSpec(shape, lambda *_: (0,) * nd, pipeline_mode=pl.Buffered(1))


def _params(n_grid):
    return pltpu.CompilerParams(dimension_semantics=("arbitrary",) * n_grid,
                                vmem_limit_bytes=VMEM_LIMIT)


def _rope_tables(pos):
    half = RET_DK // 2
    inv = ROPE_BASE ** (-jnp.arange(half, dtype=F32) / half)
    ang = pos[:, None] * inv[None, :]
    cos, sin = jnp.cos(ang), jnp.sin(ang)
    return jnp.concatenate([cos, cos], axis=-1), jnp.concatenate([-sin, sin], axis=-1)


def kernel(x_prompt, x_sample, state_ret, state_gla, cache_conv, ln1, w_in, w_gate_up, b_gate_up,
           g_ret, g_gla, w_ret_out, w_gla_out, w_o, ln2, w_up, conv_w, conv_b, w_down, ln_f):
    Bp, Lp, _ = x_prompt.shape
    Bs, Ls, _ = x_sample.shape
    assert state_ret.shape[0] == 1, "single layer"
    assert Ls >= CONV_WIDTH - 1 and Bs % SUBLANES == 0

    c = IN_COLS
    proj_w = [ln1[0][None, :],
              w_in[0][:, :MAIN_W].astype(BF16),
              w_in[0][:, c[9]:].astype(BF16),
              jnp.pad(w_in[0][:, c[8]:c[9]], ((0, 0), (0, LANES - GLA_GATE_RANK))).astype(BF16),
              jnp.pad(w_gate_up[0], ((0, LANES - GLA_GATE_RANK), (0, 0))).astype(BF16),
              b_gate_up[0][None, :]]
    post_w = [jnp.concatenate([g_ret[0], g_gla[0]])[None, :], w_ret_out[0].astype(BF16),
              w_gla_out[0].astype(BF16), w_o[0].astype(BF16), ln2[0][None, :],
              w_up[0].astype(BF16), conv_w[0], conv_b[0][None, :], w_down[0].astype(BF16),
              ln_f[None, :]]
    proj_w_specs = [_const_spec(w.shape) for w in proj_w]
    post_w_specs = [_const_spec(w.shape) for w in post_w]
    cos_p, sin_p = _rope_tables(jnp.arange(Lp, dtype=F32))
    cos_s, sin_s = _rope_tables(PAST_LEN + jnp.arange(Ls, dtype=F32))
    sds = lambda *shape: jax.ShapeDtypeStruct(shape, F32)
    full = lambda *shape: pl.BlockSpec(shape, lambda *_: (0,) * len(shape))

    T = 256
    grid_p = (Bp, Lp // T)
    rows_p = lambda w: pl.BlockSpec((None, T, w), lambda b, t: (b, t, 0))
    state_p = lambda *shape: pl.BlockSpec((None,) + shape, lambda b, t: (b,) + (0,) * len(shape))
    rope_spec = pl.BlockSpec((T, RET_DK), lambda b, t: (t, 0))
    y_p, tail_p, sret_p, sgla_p = pl.pallas_call(
        _layer_prompt_kernel, grid=grid_p,
        in_specs=[rows_p(D_MODEL), rope_spec, rope_spec] + proj_w_specs + post_w_specs,
        out_specs=[rows_p(D_MODEL), state_p(CONV_WIDTH - 1, D_FF),
                   state_p(RET_HEADS, RET_DK, RET_DV), state_p(GLA_PAIRS, PAIR_DK, GLA_DV)],
        out_shape=[sds(Bp, Lp, D_MODEL), sds(Bp, CONV_WIDTH - 1, D_FF),
                   sds(Bp, RET_HEADS, RET_DK, RET_DV), sds(Bp, GLA_PAIRS, PAIR_DK, GLA_DV)],
        scratch_shapes=[
            pltpu.VMEM((T, SCAN_W), F32), pltpu.VMEM((T, GLA_QK_W), F32),
            pltpu.VMEM((T, O_W), F32), pltpu.VMEM((T, O_W), F32),
            pltpu.VMEM((T, 2 * D_MODEL), F32), pltpu.VMEM((T, O_W), BF16),
            pltpu.VMEM((T // RET_CHUNK, RET_HEADS, RET_DK + RET_CHUNK, RET_DV), BF16),
            pltpu.VMEM((T // CHUNK, GLA_PAIRS, PAIR_DK + 2 * CHUNK, GLA_DV), BF16),
            pltpu.VMEM((T + SUBLANES, D_FF), F32)],
        compiler_params=pltpu.CompilerParams(
            dimension_semantics=("parallel", "arbitrary"), vmem_limit_bytes=VMEM_LIMIT),
        name="layer_prompt",
    )(x_prompt, cos_p, sin_p, *proj_w, *post_w)

    scan_s, gate_s, la_s = pl.pallas_call(
        _proj_sample_kernel, grid=(1,),
        in_specs=[_const_spec((Bs, Ls, D_MODEL)), full(Ls, RET_DK), full(Ls, RET_DK)]
        + proj_w_specs,
        out_specs=[full(Bs, Ls * SCAN_W), full(Bs, Ls * GATE_W), full(Bs, Ls * GLA_QK_W)],
        out_shape=[sds(Bs, Ls * SCAN_W), sds(Bs, Ls * GATE_W), sds(Bs, Ls * GLA_QK_W)],
        compiler_params=_params(1), name="proj_sample",
    )(x_sample, cos_s, sin_s, *proj_w)

    SB = 16
    rows_s = lambda w: pl.BlockSpec((SB, w), lambda i: (i, 0))
    state_s = lambda *shape: pl.BlockSpec((SB,) + shape, lambda i: (i,) + (0,) * len(shape))
    o_s, sret_s, sgla_s = pl.pallas_call(
        functools.partial(_scan_sample_kernel, Ls), grid=(Bs // SB,),
        in_specs=[rows_s(Ls * SCAN_W), rows_s(Ls * GLA_QK_W),
                  state_s(RET_HEADS, RET_DK, RET_DV), state_s(GLA_PAIRS, PAIR_DK, GLA_DV)],
        out_specs=[rows_s(Ls * O_W), state_s(RET_HEADS, RET_DK, RET_DV),
                   state_s(GLA_PAIRS, PAIR_DK, GLA_DV)],
        out_shape=[sds(Bs, Ls * O_W), sds(Bs, RET_HEADS, RET_DK, RET_DV),
                   sds(Bs, GLA_PAIRS, PAIR_DK, GLA_DV)],
        compiler_params=_params(1), name="scan_sample",
    )(scan_s, la_s, state_ret[0], state_gla[0].reshape(Bs, GLA_PAIRS, PAIR_DK, GLA_DV))

    y_s, tail_s = pl.pallas_call(
        _post_sample_kernel, grid=(1,),
        in_specs=[_const_spec(s) for s in ((Bs, Ls * O_W), (Bs, Ls * GATE_W), (Bs, Ls, D_MODEL),
                                           (Bs, CONV_WIDTH - 1, D_FF))] + post_w_specs,
        out_specs=[full(Bs, Ls, D_MODEL), full(Bs, CONV_WIDTH - 1, D_FF)],
        out_shape=[sds(Bs, Ls, D_MODEL), sds(Bs, CONV_WIDTH - 1, D_FF)],
        scratch_shapes=[pltpu.VMEM((CONV_WIDTH - 1, Bs, D_FF), F32)],
        compiler_params=_params(1), name="post_sample",
    )(o_s, gate_s, x_sample, cache_conv[0], *post_w)

    gshape = (1, -1, GLA_HEADS, GLA_DK, GLA_DV)
    return (y_p, y_s, sret_p[None], sret_s[None], sgla_p.reshape(gshape), sgla_s.reshape(gshape),
            tail_p[None], tail_s[None])
```
